```python
import jax
import jax.numpy as jnp
from jax import lax
import numpy as np


D_MODEL = 1024
BATCH = 8
SEQ = 2048
DEPTH = 4

N_BRANCH = 4
BRANCH_WIDTH = 512
RWKV_HEADS = 8
RWKV_HEAD_DIM = BRANCH_WIDTH // RWKV_HEADS
RWKV_DECAY_RANK = 64
RWKV_ICL_RANK = 64
RWKV_VALUE_RANK = 32
RWKV_GATE_RANK = 128
RWKV_LNX_EPS = 64e-5
HGRN_HEADS = 4
HGRN_HEAD_DIM = BRANCH_WIDTH // HGRN_HEADS
MLSTM_HEADS = 4
MLSTM_HEAD_DIM = BRANCH_WIDTH // MLSTM_HEADS
CHUNK = 64
LRU_BLOCKS = 8
LRU_BLOCK_DIM = BRANCH_WIDTH // LRU_BLOCKS
LRU_C = 8.0
CONV_WIDTH = 4
N_EXPERTS = 32
TOP_K = 4
D_EXPERT = 1024
SWIGLU_LIMIT = 7.0
SWIGLU_ALPHA = 1.702
EXPERT_BLOCK = 128
D_PLE = 256
LN_EPS = 1e-5
NORM_EPS = 1e-6
DEEPNORM_ALPHA = (2 * DEPTH) ** 0.25
DEEPNORM_BETA = (8 * DEPTH) ** -0.25

RWKV_SPLITS = (BRANCH_WIDTH, BRANCH_WIDTH, BRANCH_WIDTH, RWKV_DECAY_RANK, RWKV_ICL_RANK, RWKV_GATE_RANK)
RWKV_COLS = 3 * BRANCH_WIDTH + RWKV_DECAY_RANK + RWKV_ICL_RANK + RWKV_GATE_RANK
HGRN_COLS = 4 * BRANCH_WIDTH
MLSTM_SPLITS = (BRANCH_WIDTH, BRANCH_WIDTH, BRANCH_WIDTH, BRANCH_WIDTH, MLSTM_HEADS, MLSTM_HEADS)
MLSTM_COLS = 4 * BRANCH_WIDTH + 2 * MLSTM_HEADS
LRU_COLS = 2 * BRANCH_WIDTH
GATE_COLS = N_BRANCH * D_MODEL
N_COLS_BASE = RWKV_COLS + HGRN_COLS + MLSTM_COLS + LRU_COLS + GATE_COLS
N_COLS_REST = N_COLS_BASE + RWKV_VALUE_RANK

F32 = jnp.float32

kernel_name = 'hybrid_rwkv7_hgrn2_mlstm_rglru_moe_deepnorm'


def _split(z, sizes):
    return jnp.split(z, np.cumsum(sizes)[:-1].tolist(), axis=-1)


def _layernorm(x, g, b, dtype):
    xf = x.astype(F32)
    xc = xf - jnp.mean(xf, -1, keepdims=True)
    var = jnp.mean(xc * xc, -1, keepdims=True)
    return (xc * lax.rsqrt(var + LN_EPS) * g + b).astype(dtype)


def _head_layernorm(y, eps):
    yc = y - jnp.mean(y, -1, keepdims=True)
    out = yc * lax.rsqrt(jnp.mean(yc * yc, -1, keepdims=True) + eps)
    return out.reshape(out.shape[:-2] + (-1,))


def _head_rmsnorm(y, eps):
    out = y * lax.rsqrt(jnp.mean(y * y, -1, keepdims=True) + eps)
    return out.reshape(out.shape[:-2] + (-1,))


def _token_shift(z):
    return jnp.pad(z, ((0, 0), (1, 0), (0, 0)))[:, :-1]


def _causal_conv(z, w, b):
    seq = z.shape[1]
    zp = jnp.pad(z, ((0, 0), (CONV_WIDTH - 1, 0), (0, 0)))
    out = b
    for j in range(CONV_WIDTH):
        out = out + zp[:, j:j + seq] * w[j]
    return out


def _to_heads(t, n_heads):
    bsz, seq, width = t.shape
    return t.reshape(bsz, seq, n_heads, width // n_heads).transpose(0, 2, 1, 3)


def _from_heads(t):
    return t.transpose(0, 2, 1, 3)


def _to_chunks(t):
    bsz, nh, seq = t.shape[:3]
    t = t.reshape((bsz, nh, seq // CHUNK, CHUNK) + t.shape[3:])
    return jnp.moveaxis(t, 2, 0)


def _from_chunks(t):
    nc, bsz, nh, c, dv = t.shape
    return jnp.moveaxis(t, 0, 2).reshape(bsz, nh, nc * c, dv)


def rwkv7_mix(zA, v_first, v_mix, mu, w0, w2, a0, a2, g2, k_k, k_a, r_k, lnx_g, lnx_b):
    bsz, seq = zA.shape[:2]
    zA = zA + (_token_shift(zA) - zA) * mu
    r, k, v, xw, xa, xg = _split(zA, RWKV_SPLITS)
    w_log = -jax.nn.softplus(-(w0 + jnp.tanh(xw) @ w2)) - 0.5
    decay = jnp.exp(-jnp.exp(w_log))
    a = jax.nn.sigmoid(a0 + xa @ a2)
    g = jax.nn.sigmoid(xg) @ g2
    if v_mix is not None:
        v = v + (v_first - v) * jax.nn.sigmoid(v_mix)

    def hs(t):
        return t.reshape(bsz, seq, RWKV_HEADS, RWKV_HEAD_DIM)

    kk = hs(k * k_k)
    kk = kk / jnp.maximum(jnp.sqrt(jnp.sum(kk * kk, -1, keepdims=True)), 1e-12)
    k = k * (1.0 + (a - 1.0) * k_a)
    rh, kh, vh, dh, ah = hs(r), hs(k), hs(v), hs(decay), hs(a)
    neg_kk = -kk
    kk_a = kk * ah

    def step(state, inp):
        r_t, w_t, k_t, v_t, a_t, b_t = inp
        sa = jnp.einsum('bhvk,bhk->bhv', state, a_t)
        state = state * w_t[:, :, None, :] + sa[..., None] * b_t[:, :, None, :] + v_t[..., None] * k_t[:, :, None, :]
        return state, jnp.einsum('bhvk,bhk->bhv', state, r_t)

    xs = tuple(jnp.moveaxis(t, 1, 0) for t in (rh, dh, kh, vh, neg_kk, kk_a))
    state0 = jnp.zeros((bsz, RWKV_HEADS, RWKV_HEAD_DIM, RWKV_HEAD_DIM), F32)
    _, y = lax.scan(step, state0, xs)
    y = jnp.moveaxis(y, 0, 1)
    y = _head_layernorm(y, RWKV_LNX_EPS) * lnx_g + lnx_b
    bonus = (jnp.sum(rh * kh * r_k, -1, keepdims=True) * vh).reshape(bsz, seq, BRANCH_WIDTH)
    return (y + bonus) * g, v


def _gla_chunked(q, k, v, log_f):
    bsz, nh, _, dk = q.shape
    dv = v.shape[-1]
    causal = jnp.tril(jnp.ones((CHUNK, CHUNK), bool))

    def chunk(state, inp):
        qc, kc, vc, lf = inp
        bcum = jnp.cumsum(lf, axis=-2)
        o_inter = jnp.einsum('bhtd,bhde->bhte', qc * jnp.exp(bcum), state)
        diff = bcum[:, :, :, None, :] - bcum[:, :, None, :, :]
        pair_decay = jnp.exp(jnp.where(causal[:, :, None], diff, -jnp.inf))
        attn = jnp.einsum('bhtd,bhsd,bhtsd->bhts', qc, kc, pair_decay)
        o = o_inter + jnp.einsum('bhts,bhse->bhte', attn, vc)
        b_last = bcum[:, :, -1]
        state = jnp.exp(b_last)[..., None] * state + jnp.einsum('bhsd,bhse->bhde', kc * jnp.exp(b_last[:, :, None] - bcum), vc)
        return state, o

    state0 = jnp.zeros((bsz, nh, dk, dv), F32)
    _, o = lax.scan(chunk, state0, tuple(_to_chunks(t) for t in (q, k, v, log_f)))
    return _from_chunks(o)


def hgrn2_mix(zB, lb, norm_w):
    q, f, i, g = _split(zB, (BRANCH_WIDTH,) * 4)
    q = jax.nn.silu(q)
    log_f = jnp.logaddexp(jnp.log(lb), jnp.log1p(-lb) + jax.nn.log_sigmoid(f))
    k = (1.0 - lb) * jax.nn.sigmoid(-f)
    o = _gla_chunked(*(_to_heads(t, HGRN_HEADS) for t in (q, k, i, log_f)))
    o = _head_rmsnorm(_from_heads(o), NORM_EPS) * norm_w
    return o * jax.nn.silu(g)


def _mlstm_chunked(q, k, v, i_pre, log_f):
    bsz, nh, _, dk = q.shape
    dv = v.shape[-1]
    causal = jnp.tril(jnp.ones((CHUNK, CHUNK), bool))

    def chunk(carry, inp):
        c_mat, n_vec, m_prev = carry
        qc, kc, vc, ic, lf = inp
        fcum = jnp.cumsum(lf, axis=-1)
        log_w = jnp.where(causal, fcum[..., :, None] - fcum[..., None, :] + ic[..., None, :], -jnp.inf)
        log_inter = fcum + m_prev[..., None]
        m_t = jnp.maximum(log_inter, jnp.max(log_w, -1))
        w_intra = jnp.exp(log_w - m_t[..., None])
        w_inter = jnp.exp(log_inter - m_t)
        s_qk = jnp.einsum('bhtd,bhsd->bhts', qc, kc) * w_intra
        num = w_inter[..., None] * jnp.einsum('bhtd,bhde->bhte', qc, c_mat) + jnp.einsum('bhts,bhse->bhte', s_qk, vc)
        den = w_inter * jnp.einsum('bhtd,bhd->bht', qc, n_vec) + jnp.sum(s_qk, -1)
        h = num / jnp.maximum(jnp.abs(den), jnp.exp(-m_t))[..., None]
        f_last = fcum[..., -1]
        log_s = f_last[..., None] - fcum + ic
        m_new = jnp.maximum(f_last + m_prev, jnp.max(log_s, -1))
        ws = jnp.exp(log_s - m_new[..., None])
        carry_decay = jnp.exp(f_last + m_prev - m_new)
        c_mat = carry_decay[..., None, None] * c_mat + jnp.einsum('bhs,bhsd,bhse->bhde', ws, kc, vc)
        n_vec = carry_decay[..., None] * n_vec + jnp.einsum('bhs,bhsd->bhd', ws, kc)
        return (c_mat, n_vec, m_new), h

    carry0 = (jnp.zeros((bsz, nh, dk, dv), F32), jnp.zeros((bsz, nh, dk), F32), jnp.full((bsz, nh), -1e30, F32))
    _, h = lax.scan(chunk, carry0, tuple(_to_chunks(t) for t in (q, k, v, i_pre, log_f)))
    return _from_chunks(h)


def mlstm_mix(zC, conv_w, conv_b, i_bias, f_bias, norm_w):
    q, k, v, o, ig, fg = _split(zC, MLSTM_SPLITS)
    qk = jax.nn.silu(_causal_conv(jnp.concatenate([q, k], -1), conv_w, conv_b))
    q, k = _split(qk, (BRANCH_WIDTH, BRANCH_WIDTH))
    qh = _to_heads(q, MLSTM_HEADS)
    kh = _to_heads(k, MLSTM_HEADS) * MLSTM_HEAD_DIM ** -0.5
    vh = _to_heads(v, MLSTM_HEADS)
    i_pre = (ig + i_bias).transpose(0, 2, 1)
    log_f = jax.nn.log_sigmoid(fg + f_bias).transpose(0, 2, 1)
    h = _mlstm_chunked(qh, kh, vh, i_pre, log_f)
    h = _head_layernorm(_from_heads(h), NORM_EPS) * norm_w
    return jax.nn.sigmoid(o) * h


def _linear_combine(left, right):
    a_l, b_l = left
    a_r, b_r = right
    return a_l * a_r, a_r * b_l + b_r


def rglru_mix(zD, conv_w, conv_b, wx, bx, wa, ba, lam):
    xr, gate = _split(zD, (BRANCH_WIDTH, BRANCH_WIDTH))
    xc = _causal_conv(xr, conv_w, conv_b)
    bsz, seq = xc.shape[:2]
    xb = xc.reshape(bsz, seq, LRU_BLOCKS, LRU_BLOCK_DIM)
    gate_x = jax.nn.sigmoid(jnp.einsum('bsgi,gij->bsgj', xb, wx) + bx).reshape(bsz, seq, BRANCH_WIDTH)
    gate_a = jax.nn.sigmoid(jnp.einsum('bsgi,gij->bsgj', xb, wa) + ba).reshape(bsz, seq, BRANCH_WIDTH)
    log_a = LRU_C * gate_a * jax.nn.log_sigmoid(lam)
    mult = jnp.sqrt(-jnp.expm1(2.0 * log_a)).at[:, 0].set(1.0)
    _, h = lax.associative_scan(_linear_combine, (jnp.exp(log_a), mult * gate_x * xc), axis=1)
    return h * jax.nn.gelu(gate)


def _expert_ffn(xb, w_gu, b_gu, w_down, b_down):
    gu = xb @ w_gu + b_gu
    gate, up = gu[:, :D_EXPERT], gu[:, D_EXPERT:]
    gate = jnp.minimum(gate, SWIGLU_LIMIT)
    up = jnp.clip(up, -SWIGLU_LIMIT, SWIGLU_LIMIT)
    glu = gate * jax.nn.sigmoid(SWIGLU_ALPHA * gate)
    return ((up + 1.0) * glu) @ w_down + b_down


def moe(xf, router_w, router_b, w_gu, b_gu, w_down, b_down):
    n_tok = xf.shape[0]
    logits = (xf @ router_w + router_b).astype(F32)
    top_val, top_idx = lax.top_k(logits, TOP_K)
    gate = jax.nn.softmax(top_val, axis=-1)
    n_assign = n_tok * TOP_K
    n_blocks = -(-n_assign // EXPERT_BLOCK) + N_EXPERTS
    n_slots = n_blocks * EXPERT_BLOCK
    e_flat = top_idx.reshape(-1)
    order = jnp.argsort(e_flat)
    e_sorted = e_flat[order]
    counts = jnp.bincount(e_flat, length=N_EXPERTS)
    padded = (counts + EXPERT_BLOCK - 1) // EXPERT_BLOCK * EXPERT_BLOCK
    start = jnp.cumsum(counts) - counts
    pad_end = jnp.cumsum(padded)
    pad_start = pad_end - padded
    dest = pad_start[e_sorted] + jnp.arange(n_assign) - start[e_sorted]
    slot_tok = jnp.zeros((n_slots,), jnp.int32).at[dest].set((order // TOP_K).astype(jnp.int32))
    slot_w = jnp.zeros((n_slots,), F32).at[dest].set(gate.reshape(-1)[order])
    block_e = jnp.minimum(jnp.searchsorted(pad_end, jnp.arange(n_blocks) * EXPERT_BLOCK, side='right'), N_EXPERTS - 1)
    xs = xf[slot_tok].reshape(n_blocks, EXPERT_BLOCK, -1)

    def run_block(args):
        xb, e = args
        return _expert_ffn(xb, w_gu[e], b_gu[e], w_down[e], b_down[e])

    ys = lax.map(run_block, (xs, block_e)).reshape(n_slots, -1)
    return jnp.zeros(xf.shape, F32).at[slot_tok].add(ys.astype(F32) * slot_w[:, None])


def setup_inputs(seed: int = 0) -> dict:
    key = jax.random.key(seed)
    ks = iter(jax.random.split(key, 64))
    L, W, D = DEPTH, BRANCH_WIDTH, D_MODEL

    def nrm(shape, scale):
        return scale * jax.random.normal(next(ks), shape, F32)

    def uni(shape, lo, hi):
        return jax.random.uniform(next(ks), shape, F32, lo, hi)

    return {
        'x': nrm((BATCH, SEQ, D), 1.0),
        'p': nrm((L, BATCH, SEQ, D_PLE), 1.0),
        'w_in_first': nrm((D, N_COLS_BASE), D ** -0.5),
        'w_in_rest': nrm((L - 1, D, N_COLS_REST), D ** -0.5),
        'rwkv_mu': uni((L, RWKV_COLS), 0.0, 1.0),
        'rwkv_w0': uni((L, W), -6.0, 0.0),
        'rwkv_w2': nrm((L, RWKV_DECAY_RANK, W), 0.5 * RWKV_DECAY_RANK ** -0.5),
        'rwkv_a0': nrm((L, W), 0.1),
        'rwkv_a2': nrm((L, RWKV_ICL_RANK, W), 0.5 * RWKV_ICL_RANK ** -0.5),
        'rwkv_v0': nrm((L - 1, W), 0.1),
        'rwkv_v2': nrm((L - 1, RWKV_VALUE_RANK, W), 0.5 * RWKV_VALUE_RANK ** -0.5),
        'rwkv_g2': nrm((L, RWKV_GATE_RANK, W), RWKV_GATE_RANK ** -0.5),
        'rwkv_k_k': 0.85 + nrm((L, W), 0.05),
        'rwkv_k_a': 1.0 + nrm((L, W), 0.05),
        'rwkv_r_k': nrm((L, RWKV_HEADS, RWKV_HEAD_DIM), 0.1),
        'rwkv_lnx_g': 1.0 + nrm((L, W), 0.05),
        'rwkv_lnx_b': nrm((L, W), 0.02),
        'hgrn_lower_bounds': nrm((L, W), 1.0),
        'hgrn_norm_w': 1.0 + nrm((L, W), 0.05),
        'mlstm_conv_w': nrm((L, CONV_WIDTH, 2 * W), CONV_WIDTH ** -0.5),
        'mlstm_conv_b': nrm((L, 2 * W), 0.02),
        'mlstm_i_bias': nrm((L, MLSTM_HEADS), 0.1),
        'mlstm_f_bias': jnp.linspace(3.0, 6.0, MLSTM_HEADS, dtype=F32) + nrm((L, MLSTM_HEADS), 0.1),
        'mlstm_norm_w': 1.0 + nrm((L, W), 0.05),
        'lru_conv_w': nrm((L, CONV_WIDTH, W), CONV_WIDTH ** -0.5),
        'lru_conv_b': nrm((L, W), 0.02),
        'lru_wx': nrm((L, LRU_BLOCKS, LRU_BLOCK_DIM, LRU_BLOCK_DIM), LRU_BLOCK_DIM ** -0.5),
        'lru_bx': nrm((L, LRU_BLOCKS, LRU_BLOCK_DIM), 0.02),
        'lru_wa': nrm((L, LRU_BLOCKS, LRU_BLOCK_DIM, LRU_BLOCK_DIM), LRU_BLOCK_DIM ** -0.5),
        'lru_ba': nrm((L, LRU_BLOCKS, LRU_BLOCK_DIM), 0.02),
        'lru_lambda': uni((L, W), 3.5, 9.0),
        'w_branch': nrm((L, N_BRANCH, W, D), W ** -0.5),
        'w_out': nrm((L, D, D), DEEPNORM_BETA * D ** -0.5),
        'ln1_g': 1.0 + nrm((L, D), 0.05),
        'ln1_b': nrm((L, D), 0.02),
        'router_w': nrm((L, D, N_EXPERTS), D ** -0.5),
        'router_b': nrm((L, N_EXPERTS), 0.01),
        'expert_w_gu': nrm((L, N_EXPERTS, D, 2 * D_EXPERT), D ** -0.5),
        'expert_b_gu': nrm((L, N_EXPERTS, 2 * D_EXPERT), 0.02),
        'expert_w_down': nrm((L, N_EXPERTS, D_EXPERT, D), DEEPNORM_BETA * D_EXPERT ** -0.5),
        'expert_b_down': nrm((L, N_EXPERTS, D), 0.02),
        'ple_gate_w': nrm((L, D, D), D ** -0.5),
        'ple_proj_w': nrm((L, D_PLE, D), DEEPNORM_BETA * D_PLE ** -0.5),
        'ln2_g': 1.0 + nrm((L, D), 0.05),
        'ln2_b': nrm((L, D), 0.02),
    }


def reference(x, p, w_in_first, w_in_rest, rwkv_mu, rwkv_w0, rwkv_w2, rwkv_a0, rwkv_a2, rwkv_v0, rwkv_v2,
              rwkv_g2, rwkv_k_k, rwkv_k_a, rwkv_r_k, rwkv_lnx_g, rwkv_lnx_b, hgrn_lower_bounds, hgrn_norm_w,
              mlstm_conv_w, mlstm_conv_b, mlstm_i_bias, mlstm_f_bias, mlstm_norm_w, lru_conv_w, lru_conv_b,
              lru_wx, lru_bx, lru_wa, lru_ba, lru_lambda, w_branch, w_out, ln1_g, ln1_b, router_w, router_b,
              expert_w_gu, expert_b_gu, expert_w_down, expert_b_down, ple_gate_w, ple_proj_w, ln2_g, ln2_b):
    dt = x.dtype
    bsz, seq, _ = x.shape
    lb_all = jnp.cumsum(jax.nn.softmax(hgrn_lower_bounds.astype(F32), axis=0), axis=0)
    lb_all = lb_all - lb_all[0]
    v_first = None
    for layer in range(DEPTH):
        w_in = w_in_first if layer == 0 else w_in_rest[layer - 1]
        z = (x @ w_in).astype(F32)
        zA, zB, zC, zD, zG = _split(z[..., :N_COLS_BASE], (RWKV_COLS, HGRN_COLS, MLSTM_COLS, LRU_COLS, GATE_COLS))
        v_mix = None if layer == 0 else rwkv_v0[layer - 1] + z[..., N_COLS_BASE:] @ rwkv_v2[layer - 1]
        yA, v_cur = rwkv7_mix(zA, v_first, v_mix, rwkv_mu[layer], rwkv_w0[layer], rwkv_w2[layer], rwkv_a0[layer],
                              rwkv_a2[layer], rwkv_g2[layer], rwkv_k_k[layer], rwkv_k_a[layer], rwkv_r_k[layer],
                              rwkv_lnx_g[layer], rwkv_lnx_b[layer])
        if layer == 0:
            v_first = v_cur
        yB = hgrn2_mix(zB, lb_all[layer], hgrn_norm_w[layer])
        yC = mlstm_mix(zC, mlstm_conv_w[layer], mlstm_conv_b[layer], mlstm_i_bias[layer], mlstm_f_bias[layer],
                       mlstm_norm_w[layer])
        yD = rglru_mix(zD, lru_conv_w[layer], lru_conv_b[layer], lru_wx[layer], lru_bx[layer], lru_wa[layer],
                       lru_ba[layer], lru_lambda[layer])
        gates = jax.nn.sigmoid(zG).reshape(bsz, seq, N_BRANCH, D_MODEL)
        merged = jnp.zeros((bsz, seq, D_MODEL), F32)
        for n, yb in enumerate((yA, yB, yC, yD)):
            merged = merged + gates[:, :, n] * (yb @ w_branch[layer, n])
        mix = merged @ w_out[layer]
        x = _layernorm(DEEPNORM_ALPHA * x + mix, ln1_g[layer], ln1_b[layer], dt)
        moe_out = moe(x.reshape(bsz * seq, D_MODEL), router_w[layer], router_b[layer], expert_w_gu[layer],
                      expert_b_gu[layer], expert_w_down[layer], expert_b_down[layer]).reshape(bsz, seq, D_MODEL)
        ple = jax.nn.sigmoid(x @ ple_gate_w[layer]) * (p[layer] @ ple_proj_w[layer])
        x = _layernorm(DEEPNORM_ALPHA * x + moe_out + ple, ln2_g[layer], ln2_b[layer], dt)
    return x
```

```python
import functools
import math

import jax
import jax.numpy as jnp
from jax import lax
from jax.experimental import pallas as pl
from jax.experimental.pallas import tpu as pltpu

F32 = jnp.float32
BF16 = jnp.bfloat16
I32 = jnp.int32

D_MODEL = 1024
DEPTH = 4
W = 512
RWKV_HEAD_DIM = 64
RWKV_LNX_EPS = 64e-5
MLSTM_HEADS = 4
HEAD128 = 128
CHUNK = 64
LRU_C = 8.0
CONV_WIDTH = 4
N_EXPERTS = 32
TOP_K = 4
D_EXPERT = 1024
SWIGLU_LIMIT = 7.0
SWIGLU_ALPHA = 1.702
D_PLE = 256
LN_EPS = 1e-5
NORM_EPS = 1e-6
ALPHA = (2 * DEPTH) ** 0.25

COL_G = 0
COL_B = 4096
COL_C = 6144
COL_A = 8192
COL_D = 10240
N_COLS = 11264
A_MAIN = 1792
SMALL_VR = 0
SMALL_IG = 32
SMALL_FG = 36

EXPERT_ROWS = 256
TOK_TILE = 256
LANES = 128
VMEM_LIMIT = 48 * 1024 * 1024


def _cparams(sem):
    return pltpu.CompilerParams(dimension_semantics=sem, vmem_limit_bytes=VMEM_LIMIT)


def _mm(a, b):
    return jnp.dot(a.astype(BF16), b.astype(BF16), preferred_element_type=F32)


def _mm_nt(a, b):
    return lax.dot_general(a.astype(BF16), b.astype(BF16), (((1,), (1,)), ((), ())),
                           preferred_element_type=F32)


def _mm_tn(a, b):
    return jnp.dot(a.T.astype(BF16), b.astype(BF16), preferred_element_type=F32)


def _split3(x):
    hi = x.astype(BF16)
    r1 = x - hi.astype(F32)
    mid = r1.astype(BF16)
    lo = (r1 - mid.astype(F32)).astype(BF16)
    return hi, mid, lo


def _mm_exact_lhs(m, x):
    hi, mid, lo = _split3(x)
    return (jnp.dot(m, hi, preferred_element_type=F32) + jnp.dot(m, mid, preferred_element_type=F32)
            + jnp.dot(m, lo, preferred_element_type=F32))


def _mm_exact_rhs(x, m):
    hi, mid, lo = _split3(x)
    return (jnp.dot(hi, m, preferred_element_type=F32) + jnp.dot(mid, m, preferred_element_type=F32)
            + jnp.dot(lo, m, preferred_element_type=F32))


def _iota(shape, dim):
    return lax.broadcasted_iota(I32, shape, dim)


def _tri_incl(n):
    return jnp.where(_iota((n, n), 1) <= _iota((n, n), 0), 1.0, 0.0).astype(BF16)


def _seg_ones(n, seg):
    return jnp.where(_iota((n, n), 0) // seg == _iota((n, n), 1) // seg, 1.0, 0.0).astype(BF16)


def _sigmoid(x):
    return jax.nn.sigmoid(x)


def _silu(x):
    return x * jax.nn.sigmoid(x)


def _log_sigmoid(x):
    return jnp.minimum(x, 0.0) - jnp.log1p(jnp.exp(-jnp.abs(x)))


def _layernorm_rows(x, g, b):
    xc = x - jnp.mean(x, -1, keepdims=True)
    var = jnp.mean(xc * xc, -1, keepdims=True)
    return xc * lax.rsqrt(var + LN_EPS) * g + b


def _proj_kernel(x_ref, w_ref, o_ref):
    o_ref[...] = jnp.dot(x_ref[...], w_ref[...], preferred_element_type=F32)


def _proj(xb, w):
    n, k = xb.shape
    c = w.shape[1]
    tm = min(1024, n)
    tn = 1024
    return pl.pallas_call(
        _proj_kernel,
        grid=(c // tn, n // tm),
        in_specs=[pl.BlockSpec((tm, k), lambda j, i: (i, 0)),
                  pl.BlockSpec((k, tn), lambda j, i: (0, j))],
        out_specs=pl.BlockSpec((tm, tn), lambda j, i: (i, j)),
        out_shape=jax.ShapeDtypeStruct((n, c), F32),
        compiler_params=_cparams(("parallel", "parallel")),
        name="in_proj",
    )(xb, w)


def _rwkv_kernel(first, z_ref, vf_ref, mu_ref, w0_ref, w2_ref, a0_ref, a2_ref, v0_ref, v2_ref, g2_ref,
                 kk_ref, ka_ref, rk_ref, lng_ref, lnb_ref, y_ref, vo_ref, prev_ref, s_ref):
    t_len = CHUNK

    @pl.when(pl.program_id(1) == 0)
    def _():
        prev_ref[...] = jnp.zeros_like(prev_ref)
        s_ref[...] = jnp.zeros_like(s_ref)

    z = z_ref[:, 0:A_MAIN]
    small = z_ref[:, A_MAIN:A_MAIN + LANES]
    row = _iota((t_len, 1), 0)
    zs = jnp.where(row == 0, prev_ref[...], pltpu.roll(z, 1, 0))
    prev_ref[...] = z[t_len - 1:t_len, :]
    zz = z + (zs - z) * mu_ref[...]

    r = zz[:, 0:W]
    k = zz[:, W:2 * W]
    v = zz[:, 2 * W:3 * W]
    xwa = zz[:, 3 * W:3 * W + LANES]
    xg = zz[:, 3 * W + LANES:3 * W + 2 * LANES]

    u = w0_ref[...] + _mm(jnp.tanh(xwa), w2_ref[...])
    logw = -math.exp(-0.5) * _sigmoid(u)
    a_sig = _sigmoid(a0_ref[...] + _mm(xwa, a2_ref[...]))
    g = _mm(_sigmoid(xg), g2_ref[...])
    if not first:
        v = v + (vf_ref[...] - v) * _sigmoid(v0_ref[...] + _mm(small, v2_ref[...]))
    vo_ref[...] = v

    seg = _seg_ones(LANES, RWKV_HEAD_DIM)
    tri = _tri_incl(t_len)
    c = _mm_exact_lhs(tri, logw)
    c_last = c[t_len - 1:t_len, :]
    gam = jnp.exp(c)
    igam = jnp.exp(-c)
    gam_ex = jnp.exp(c - logw)
    tail = jnp.exp(c_last - c)
    g_last = jnp.exp(c_last)

    col64 = _iota((t_len, t_len), 1)
    row64 = _iota((t_len, t_len), 0)
    strict = col64 < row64
    incl = col64 <= row64
    lane = _iota((1, LANES), 1)
    bd_mask = _iota((LANES, LANES), 0) // RWKV_HEAD_DIM == _iota((LANES, LANES), 1) // RWKV_HEAD_DIM

    for p in range(W // LANES):
        sl = slice(p * LANES, (p + 1) * LANES)
        kp = k[:, sl]
        kkp = kp * kk_ref[:, sl]
        ss = _mm_exact_rhs(kkp * kkp, seg)
        kkp = kkp / jnp.maximum(jnp.sqrt(ss), 1e-12)
        ap = a_sig[:, sl]
        k2 = kp * (1.0 + (ap - 1.0) * ka_ref[:, sl])
        rp = r[:, sl]
        vp = v[:, sl]
        r_t = rp * gam[:, sl]
        a_t = -kkp * gam_ex[:, sl]
        b_t = kkp * ap * igam[:, sl]
        k_t = k2 * igam[:, sl]
        b_hat = kkp * ap * tail[:, sl]
        k_hat = k2 * tail[:, sl]

        s_mat = s_ref[p]
        ah0 = _mm_nt(a_t, s_mat)
        rh0 = _mm_nt(r_t, s_mat)
        u_pair = None
        y_pair = None
        for hh in range(2):
            hmask = (lane < RWKV_HEAD_DIM) if hh == 0 else (lane >= RWKV_HEAD_DIM)
            lhs = jnp.concatenate([jnp.where(hmask, a_t, 0.0), jnp.where(hmask, r_t, 0.0)], axis=0)
            pb = _mm_nt(lhs, b_t)
            pk = _mm_nt(lhs, k_t)
            a_ab = jnp.where(strict, pb[0:t_len], 0.0)
            a_ak = jnp.where(strict, pk[0:t_len], 0.0)
            a_rb = jnp.where(incl, pb[t_len:], 0.0)
            a_rk = jnp.where(incl, pk[t_len:], 0.0)
            x = ah0 + _mm(a_ak, vp)
            pw = a_ab
            for it in range(6):
                x = x + _mm(pw, x)
                if it < 5:
                    pw = _mm(pw, pw)
            yh = rh0 + _mm(a_rb, x) + _mm(a_rk, vp)
            u_pair = x if hh == 0 else jnp.where(hmask, x, u_pair)
            y_pair = yh if hh == 0 else jnp.where(hmask, yh, y_pair)

        s_new = s_mat * g_last[:, sl] + jnp.where(bd_mask, _mm_tn(u_pair, b_hat) + _mm_tn(vp, k_hat), 0.0)
        s_ref[p] = s_new

        mean = _mm_exact_rhs(y_pair, seg) * (1.0 / RWKV_HEAD_DIM)
        yc = y_pair - mean
        var = _mm_exact_rhs(yc * yc, seg) * (1.0 / RWKV_HEAD_DIM)
        yn = yc * lax.rsqrt(var + RWKV_LNX_EPS) * lng_ref[:, sl] + lnb_ref[:, sl]
        bonus = _mm_exact_rhs(rp * k2 * rk_ref[:, sl], seg) * vp
        y_ref[:, sl] = ((yn + bonus) * g[:, sl]).astype(y_ref.dtype)


def _rwkv(z, v_first, prm, bsz, seq, first):
    n = bsz * seq
    nj = seq // CHUNK
    row_spec = lambda width: pl.BlockSpec((1, width), lambda b, j: (0, 0))
    mat_spec = lambda r_, c_: pl.BlockSpec((r_, c_), lambda b, j: (0, 0))
    tok = lambda width: pl.BlockSpec((CHUNK, width), lambda b, j: (b * nj + j, 0))
    in_specs = [
        pl.BlockSpec((CHUNK, 2048), lambda b, j: (b * nj + j, COL_A // 2048)),
        tok(W),
        row_spec(A_MAIN), row_spec(W), mat_spec(LANES, W), row_spec(W), mat_spec(LANES, W),
        row_spec(W), mat_spec(LANES, W), mat_spec(LANES, W),
        row_spec(W), row_spec(W), row_spec(W), row_spec(W), row_spec(W),
    ]
    return pl.pallas_call(
        functools.partial(_rwkv_kernel, first),
        grid=(bsz, nj),
        in_specs=in_specs,
        out_specs=[tok(W), tok(W)],
        out_shape=[jax.ShapeDtypeStruct((n, W), BF16), jax.ShapeDtypeStruct((n, W), F32)],
        scratch_shapes=[pltpu.VMEM((1, A_MAIN), F32), pltpu.VMEM((W // LANES, LANES, LANES), F32)],
        compiler_params=_cparams(("parallel", "arbitrary")),
        name="rwkv7",
    )(z, v_first, prm["mu"], prm["w0"], prm["w2"], prm["a0"], prm["a2"], prm["v0"], prm["v2"], prm["g2"],
      prm["k_k"], prm["k_a"], prm["r_k"], prm["lnx_g"], prm["lnx_b"])


HG_SUB = 16


def _hgrn_kernel(z_ref, lb_ref, nw_ref, y_ref, s_ref, q_s, k_s, v_s, bc_s, o_s):
    t_len = CHUNK

    @pl.when(pl.program_id(1) == 0)
    def _():
        s_ref[...] = jnp.zeros_like(s_ref)

    zq = z_ref[:, 0:W]
    f = z_ref[:, W:2 * W]
    lb = lb_ref[...]
    q_s[...] = _silu(zq)
    v_s[...] = z_ref[:, 2 * W:3 * W]
    la = jnp.log(lb)
    lbb = jnp.log1p(-lb) + _log_sigmoid(f)
    logf = jnp.maximum(la, lbb) + jnp.log1p(jnp.exp(-jnp.abs(la - lbb)))
    k_s[...] = (1.0 - lb) * _sigmoid(-f)
    blk = jnp.where((_iota((t_len, t_len), 1) <= _iota((t_len, t_len), 0))
                    & (_iota((t_len, t_len), 1) // HG_SUB == _iota((t_len, t_len), 0) // HG_SUB),
                    1.0, 0.0).astype(BF16)
    bc_s[...] = _mm_exact_lhs(blk, logf)

    row16 = _iota((HG_SUB, 1), 0)

    def sub_block(i, carry):
        r0 = pl.multiple_of(i * HG_SUB, HG_SUB)
        for h in range(W // HEAD128):
            sl = slice(h * HEAD128, (h + 1) * HEAD128)
            qb = q_s[pl.ds(r0, HG_SUB), sl]
            kb = k_s[pl.ds(r0, HG_SUB), sl]
            vb = v_s[pl.ds(r0, HG_SUB), sl]
            bcb = bc_s[pl.ds(r0, HG_SUB), sl]
            st = s_ref[h]
            o = _mm_nt(qb * jnp.exp(bcb), st)
            for s in range(HG_SUB):
                bs = bcb[s:s + 1, :]
                ks = kb[s:s + 1, :]
                vs = vb[s:s + 1, :]
                d = jnp.where(row16 >= s, bcb - bs, -jnp.inf)
                col = jnp.sum(qb * ks * jnp.exp(d), axis=-1, keepdims=True)
                o = o + col * vs
            bl = bcb[HG_SUB - 1:HG_SUB, :]
            s_ref[h] = st * jnp.exp(bl) + _mm_tn(vb, kb * jnp.exp(bl - bcb))
            o_s[pl.ds(r0, HG_SUB), sl] = o
        return carry

    lax.fori_loop(0, t_len // HG_SUB, sub_block, 0)

    g = z_ref[:, 3 * W:4 * W]
    for h in range(W // HEAD128):
        sl = slice(h * HEAD128, (h + 1) * HEAD128)
        o = o_s[:, sl]
        o = o * lax.rsqrt(jnp.mean(o * o, -1, keepdims=True) + NORM_EPS) * nw_ref[:, sl]
        y_ref[:, sl] = (o * _silu(g[:, sl])).astype(y_ref.dtype)


def _hgrn(z, lb, norm_w, bsz, seq):
    n = bsz * seq
    nj = seq // CHUNK
    row_spec = pl.BlockSpec((1, W), lambda b, j: (0, 0))
    return pl.pallas_call(
        _hgrn_kernel,
        grid=(bsz, nj),
        in_specs=[pl.BlockSpec((CHUNK, 2048), lambda b, j: (b * nj + j, COL_B // 2048)), row_spec, row_spec],
        out_specs=pl.BlockSpec((CHUNK, W), lambda b, j: (b * nj + j, 0)),
        out_shape=jax.ShapeDtypeStruct((n, W), BF16),
        scratch_shapes=[pltpu.VMEM((W // HEAD128, HEAD128, HEAD128), F32)]
        + [pltpu.VMEM((CHUNK, W), F32) for _ in range(5)],
        compiler_params=_cparams(("parallel", "arbitrary")),
        name="hgrn2",
    )(z, lb, norm_w)


def _mlstm_kernel(z_ref, sm_ref, cw_ref, cb_ref, gb_ref, nw_ref, y_ref, ext_ref, c_ref, n_ref, m_ref):
    t_len = CHUNK
    pad = 8

    @pl.when(pl.program_id(1) == 0)
    def _():
        ext_ref[0:pad, :] = jnp.zeros((pad, 2 * W), F32)
        c_ref[...] = jnp.zeros_like(c_ref)
        n_ref[...] = jnp.zeros_like(n_ref)
        m_ref[...] = jnp.full(m_ref.shape, -1e30, F32)

    ext_ref[pad:pad + t_len, :] = z_ref[:, 0:2 * W]
    acc = cb_ref[...]
    for jj in range(CONV_WIDTH):
        acc = acc + ext_ref[pl.ds(pad - (CONV_WIDTH - 1) + jj, t_len), :] * cw_ref[jj:jj + 1, :]
    ext_ref[0:pad, :] = z_ref[t_len - pad:t_len, 0:2 * W]
    qk = _silu(acc)
    q = qk[:, 0:W]
    k = qk[:, W:2 * W] * (HEAD128 ** -0.5)
    v = z_ref[:, 2 * W:3 * W]
    og = z_ref[:, 3 * W:4 * W]

    gates = sm_ref[...] + gb_ref[...]
    lf = _log_sigmoid(gates)
    fc = _mm_exact_lhs(_tri_incl(t_len), lf)
    gates_t = gates.T
    fc_t = fc.T
    causal = _iota((t_len, t_len), 1) <= _iota((t_len, t_len), 0)

    for h in range(MLSTM_HEADS):
        sl = slice(h * HEAD128, (h + 1) * HEAD128)
        ic_col = gates[:, SMALL_IG + h:SMALL_IG + h + 1]
        ic_row = gates_t[SMALL_IG + h:SMALL_IG + h + 1, :]
        fc_col = fc[:, SMALL_FG + h:SMALL_FG + h + 1]
        fc_row = fc_t[SMALL_FG + h:SMALL_FG + h + 1, :]
        m_prev = m_ref[h:h + 1, 0:1]
        log_w = jnp.where(causal, fc_col - fc_row + ic_row, -jnp.inf)
        log_inter = fc_col + m_prev
        m_t = jnp.maximum(log_inter, jnp.max(log_w, -1, keepdims=True))
        w_intra = jnp.exp(log_w - m_t)
        w_inter = jnp.exp(log_inter - m_t)
        qh, kh, vh = q[:, sl], k[:, sl], v[:, sl]
        s_qk = _mm_nt(qh, kh) * w_intra
        c_mat = c_ref[h]
        n_vec = n_ref[h:h + 1, :]
        num = w_inter * _mm(qh, c_mat) + _mm(s_qk, vh)
        den = w_inter * jnp.sum(qh * n_vec, -1, keepdims=True) + jnp.sum(s_qk, -1, keepdims=True)
        hid = num / jnp.maximum(jnp.abs(den), jnp.exp(-m_t))
        f_last = fc_col[t_len - 1:t_len, :]
        log_s = f_last - fc_col + ic_col
        m_new = jnp.maximum(f_last + m_prev, jnp.max(log_s, 0, keepdims=True))
        ws = jnp.exp(log_s - m_new)
        decay = jnp.exp(f_last + m_prev - m_new)
        kw = kh * ws
        c_ref[h] = decay * c_mat + _mm_tn(kw, vh)
        n_ref[h:h + 1, :] = decay * n_vec + jnp.sum(kw, 0, keepdims=True)
        m_ref[h:h + 1, :] = jnp.broadcast_to(m_new, (1, LANES))
        hc = hid - jnp.mean(hid, -1, keepdims=True)
        hn = hc * lax.rsqrt(jnp.mean(hc * hc, -1, keepdims=True) + NORM_EPS) * nw_ref[:, sl]
        y_ref[:, sl] = (_sigmoid(og[:, sl]) * hn).astype(y_ref.dtype)


def _mlstm(z, conv_w, conv_b, gate_bias, norm_w, bsz, seq):
    n = bsz * seq
    nj = seq // CHUNK
    const = lambda r_, c_: pl.BlockSpec((r_, c_), lambda b, j: (0, 0))
    return pl.pallas_call(
        _mlstm_kernel,
        grid=(bsz, nj),
        in_specs=[pl.BlockSpec((CHUNK, 2048), lambda b, j: (b * nj + j, COL_C // 2048)),
                  pl.BlockSpec((CHUNK, LANES), lambda b, j: (b * nj + j, (COL_A + A_MAIN) // LANES)),
                  const(CONV_WIDTH, 2 * W), const(1, 2 * W), const(1, LANES), const(1, W)],
        out_specs=pl.BlockSpec((CHUNK, W), lambda b, j: (b * nj + j, 0)),
        out_shape=jax.ShapeDtypeStruct((n, W), BF16),
        scratch_shapes=[pltpu.VMEM((CHUNK + 8, 2 * W), F32),
                        pltpu.VMEM((MLSTM_HEADS, HEAD128, HEAD128), F32),
                        pltpu.VMEM((8, HEAD128), F32), pltpu.VMEM((8, LANES), F32)],
        compiler_params=_cparams(("parallel", "arbitrary")),
        name="mlstm",
    )(z, z, conv_w, conv_b, gate_bias, norm_w)


LRU_T = 256


def _lru_kernel(z_ref, cw_ref, cb_ref, wx_ref, bx_ref, wa_ref, ba_ref, lam_ref, y_ref, ext_ref, h_ref):
    t_len = LRU_T
    pad = 8
    first_blk = pl.program_id(1) == 0

    @pl.when(first_blk)
    def _():
        ext_ref[0:pad, :] = jnp.zeros((pad, W), F32)
        h_ref[...] = jnp.zeros_like(h_ref)

    ext_ref[pad:pad + t_len, :] = z_ref[:, 0:W]
    xc = cb_ref[...]
    for jj in range(CONV_WIDTH):
        xc = xc + ext_ref[pl.ds(pad - (CONV_WIDTH - 1) + jj, t_len), :] * cw_ref[jj:jj + 1, :]
    ext_ref[0:pad, :] = z_ref[t_len - pad:t_len, 0:W]

    gate_x = _sigmoid(_mm(xc, wx_ref[...]) + bx_ref[...])
    gate_a = _sigmoid(_mm(xc, wa_ref[...]) + ba_ref[...])
    log_a = LRU_C * gate_a * _log_sigmoid(lam_ref[...])
    a = jnp.exp(log_a)
    mult = jnp.sqrt(-jnp.tanh(log_a) * (a * a + 1.0))
    row = _iota((t_len, 1), 0)
    mult = jnp.where(jnp.logical_and(first_blk, row == 0), 1.0, mult)
    b = mult * gate_x * xc

    d = 1
    while d < t_len:
        keep = row >= d
        a_sh = jnp.where(keep, pltpu.roll(a, d, 0), 1.0)
        b_sh = jnp.where(keep, pltpu.roll(b, d, 0), 0.0)
        b = a * b_sh + b
        a = a * a_sh
        d *= 2
    h = a * h_ref[...] + b
    h_ref[...] = h[t_len - 1:t_len, :]
    y_ref[...] = (h * jax.nn.gelu(z_ref[:, W:2 * W], approximate=True)).astype(y_ref.dtype)


def _lru(z, conv_w, conv_b, wx, bx, wa, ba, lam, bsz, seq):
    n = bsz * seq
    nj = seq // LRU_T
    const = lambda r_, c_: pl.BlockSpec((r_, c_), lambda b, j: (0, 0))
    return pl.pallas_call(
        _lru_kernel,
        grid=(bsz, nj),
        in_specs=[pl.BlockSpec((LRU_T, 2 * W), lambda b, j: (b * nj + j, COL_D // (2 * W))),
                  const(CONV_WIDTH, W), const(1, W), const(W, W), const(1, W), const(W, W), const(1, W),
                  const(1, W)],
        out_specs=pl.BlockSpec((LRU_T, W), lambda b, j: (b * nj + j, 0)),
        out_shape=jax.ShapeDtypeStruct((n, W), BF16),
        scratch_shapes=[pltpu.VMEM((LRU_T + 8, W), F32), pltpu.VMEM((1, W), F32)],
        compiler_params=_cparams(("parallel", "arbitrary")),
        name="rglru",
    )(z, conv_w, conv_b, wx, bx, wa, ba, lam)


def _merge_kernel(x_ref, ya_ref, yb_ref, yc_ref, yd_ref, zg_ref, wb_ref, wo_ref, g1_ref, b1_ref,
                  p_ref, wpg_ref, wpp_ref, x1_ref, base_ref):
    merged = None
    for nb, y_ref in enumerate((ya_ref, yb_ref, yc_ref, yd_ref)):
        proj = jnp.dot(y_ref[...], wb_ref[nb], preferred_element_type=F32)
        term = _sigmoid(zg_ref[:, nb * D_MODEL:(nb + 1) * D_MODEL]) * proj
        merged = term if merged is None else merged + term
    mix = _mm(merged, wo_ref[...])
    x1 = _layernorm_rows(ALPHA * x_ref[...] + mix, g1_ref[...], b1_ref[...])
    x1_ref[...] = x1
    ple = _sigmoid(_mm(x1, wpg_ref[...])) * _mm(p_ref[...], wpp_ref[...])
    base_ref[...] = ALPHA * x1 + ple


def _merge(x, ys, z, w_branch, w_out, g1, b1, p_l, w_pg, w_pp):
    n = x.shape[0]
    tm = TOK_TILE
    tok = lambda width: pl.BlockSpec((tm, width), lambda i: (i, 0))
    const2 = lambda r_, c_: pl.BlockSpec((r_, c_), lambda i: (0, 0))
    return pl.pallas_call(
        _merge_kernel,
        grid=(n // tm,),
        in_specs=[tok(D_MODEL), tok(W), tok(W), tok(W), tok(W),
                  pl.BlockSpec((tm, 4 * D_MODEL), lambda i: (i, COL_G // (4 * D_MODEL))),
                  pl.BlockSpec((4, W, D_MODEL), lambda i: (0, 0, 0)), const2(D_MODEL, D_MODEL),
                  const2(1, D_MODEL), const2(1, D_MODEL), tok(D_PLE), const2(D_MODEL, D_MODEL),
                  const2(D_PLE, D_MODEL)],
        out_specs=[tok(D_MODEL), tok(D_MODEL)],
        out_shape=[jax.ShapeDtypeStruct((n, D_MODEL), F32), jax.ShapeDtypeStruct((n, D_MODEL), F32)],
        compiler_params=_cparams(("parallel",)),
        name="merge_ln1",
    )(x, *ys, z, w_branch, w_out, g1, b1, p_l, w_pg, w_pp)


ROUTE_IDX = 0
ROUTE_RANK = 4
ROUTE_GATE = 8


def _route_kernel(x_ref, rw_ref, rb_ref, out_ref, cnt_ref, carry_ref):
    tm = x_ref.shape[0]

    @pl.when(pl.program_id(0) == 0)
    def _():
        carry_ref[...] = jnp.zeros_like(carry_ref)

    xh, xm, _ = _split3(x_ref[...])
    wh, wm, _ = _split3(rw_ref[...])
    logits = (jnp.dot(xh, wh, preferred_element_type=F32) + jnp.dot(xh, wm, preferred_element_type=F32)
              + jnp.dot(xm, wh, preferred_element_type=F32)) + rb_ref[...]
    lane = _iota((tm, LANES), 1)
    lane_f = lane.astype(F32)
    cur = jnp.where(lane < N_EXPERTS, logits, -jnp.inf)
    vals, idxs, hots = [], [], []
    for _k in range(TOP_K):
        m = jnp.max(cur, -1, keepdims=True)
        ik = jnp.min(jnp.where(cur == m, lane_f, float(LANES)), -1, keepdims=True)
        hot = lane_f == ik
        vals.append(m)
        idxs.append(ik)
        hots.append(hot)
        cur = jnp.where(hot, -jnp.inf, cur)
    exps = [jnp.exp(vk - vals[0]) for vk in vals]
    denom = exps[0] + exps[1] + exps[2] + exps[3]
    count = jnp.zeros((tm, LANES), F32)
    for hot in hots:
        count = count + jnp.where(hot, 1.0, 0.0)
    strict = jnp.where(_iota((tm, tm), 1) < _iota((tm, tm), 0), 1.0, 0.0).astype(BF16)
    before = jnp.dot(strict, count.astype(BF16), preferred_element_type=F32) + carry_ref[...]
    out = jnp.zeros((tm, LANES), F32)
    for kk in range(TOP_K):
        rank = jnp.sum(jnp.where(hots[kk], before, 0.0), -1, keepdims=True)
        out = jnp.where(lane == ROUTE_IDX + kk, idxs[kk], out)
        out = jnp.where(lane == ROUTE_RANK + kk, rank, out)
        out = jnp.where(lane == ROUTE_GATE + kk, exps[kk] / denom, out)
    out_ref[...] = out
    total = carry_ref[...] + jnp.sum(count, 0, keepdims=True)
    carry_ref[...] = total
    cnt_ref[...] = total


def _route(x1, rw, rb):
    n = x1.shape[0]
    tm = TOK_TILE
    return pl.pallas_call(
        _route_kernel,
        grid=(n // tm,),
        in_specs=[pl.BlockSpec((tm, D_MODEL), lambda i: (i, 0)),
                  pl.BlockSpec((D_MODEL, LANES), lambda i: (0, 0)),
                  pl.BlockSpec((1, LANES), lambda i: (0, 0))],
        out_specs=[pl.BlockSpec((tm, LANES), lambda i: (i, 0)), pl.BlockSpec((1, LANES), lambda i: (0, 0))],
        out_shape=[jax.ShapeDtypeStruct((n, LANES), F32), jax.ShapeDtypeStruct((1, LANES), F32)],
        scratch_shapes=[pltpu.VMEM((1, LANES), F32)],
        compiler_params=_cparams(("arbitrary",)),
        name="router",
    )(x1, rw, rb)


def _row_copy(src, dst, sem):
    return pltpu.make_async_copy(src, dst, sem)


def _scatter_kernel(dest_ref, x_ref, xs_in_ref, xs_ref, sem):
    del xs_in_ref
    tm = x_ref.shape[0]

    def start(n, carry):
        for kk in range(TOP_K):
            d = dest_ref[n * TOP_K + kk]
            _row_copy(x_ref.at[pl.ds(n, 1), :], xs_ref.at[pl.ds(d, 1), :], sem).start()
        return carry

    lax.fori_loop(0, tm, start, 0)

    def wait(n, carry):
        for kk in range(TOP_K):
            _row_copy(x_ref.at[pl.ds(0, 1), :], xs_ref.at[pl.ds(0, 1), :], sem).wait()
        return carry

    lax.fori_loop(0, tm, wait, 0)


def _scatter(dest, x1, n_slots):
    n = x1.shape[0]
    tm = TOK_TILE
    zeros = jnp.zeros((n_slots, D_MODEL), F32)
    return pl.pallas_call(
        _scatter_kernel,
        grid=(n // tm,),
        in_specs=[pl.BlockSpec((tm * TOP_K,), lambda i: (i,), memory_space=pltpu.SMEM),
                  pl.BlockSpec((tm, D_MODEL), lambda i: (i, 0)),
                  pl.BlockSpec(memory_space=pl.ANY)],
        out_specs=pl.BlockSpec(memory_space=pl.ANY),
        out_shape=jax.ShapeDtypeStruct((n_slots, D_MODEL), F32),
        scratch_shapes=[pltpu.SemaphoreType.DMA(())],
        input_output_aliases={2: 0},
        compiler_params=_cparams(("arbitrary",)),
        name="moe_scatter",
    )(dest, x1, zeros)


def _ffn_kernel(be_ref, nu_ref, xs_ref, wgu_ref, bgu_ref, wd_ref, bd_ref, ys_ref):
    del be_ref
    live = pl.program_id(0) < nu_ref[0]

    @pl.when(live)
    def _():
        gu = jnp.dot(xs_ref[...].astype(BF16), wgu_ref[0], preferred_element_type=F32) + bgu_ref[0]
        gate = jnp.minimum(gu[:, 0:D_EXPERT], SWIGLU_LIMIT)
        up = jnp.clip(gu[:, D_EXPERT:], -SWIGLU_LIMIT, SWIGLU_LIMIT)
        glu = gate * _sigmoid(SWIGLU_ALPHA * gate)
        act = ((up + 1.0) * glu).astype(BF16)
        ys_ref[...] = jnp.dot(act, wd_ref[0], preferred_element_type=F32) + bd_ref[0]

    @pl.when(jnp.logical_not(live))
    def _():
        ys_ref[...] = jnp.zeros_like(ys_ref)


def _ffn(block_e, n_used, xs, w_gu, b_gu, w_down, b_down):
    n_slots = xs.shape[0]
    bm = EXPERT_ROWS
    grid_spec = pltpu.PrefetchScalarGridSpec(
        num_scalar_prefetch=2,
        grid=(n_slots // bm,),
        in_specs=[pl.BlockSpec((bm, D_MODEL), lambda i, be, nu: (i, 0)),
                  pl.BlockSpec((1, D_MODEL, 2 * D_EXPERT), lambda i, be, nu: (be[i], 0, 0)),
                  pl.BlockSpec((1, 1, 2 * D_EXPERT), lambda i, be, nu: (be[i], 0, 0)),
                  pl.BlockSpec((1, D_EXPERT, D_MODEL), lambda i, be, nu: (be[i], 0, 0)),
                  pl.BlockSpec((1, 1, D_MODEL), lambda i, be, nu: (be[i], 0, 0))],
        out_specs=pl.BlockSpec((bm, D_MODEL), lambda i, be, nu: (i, 0)),
    )
    return pl.pallas_call(
        _ffn_kernel,
        grid_spec=grid_spec,
        out_shape=jax.ShapeDtypeStruct((n_slots, D_MODEL), F32),
        compiler_params=_cparams(("arbitrary",)),
        name="moe_ffn",
    )(block_e, n_used, xs, w_gu, b_gu, w_down, b_down)


def _combine_kernel(dest_ref, base_ref, route_ref, ys_ref, g2_ref, b2_ref, xo_ref, xb_ref, buf, sem):
    tm = base_ref.shape[0]

    def start(n, carry):
        for kk in range(TOP_K):
            d = dest_ref[n * TOP_K + kk]
            _row_copy(ys_ref.at[pl.ds(d, 1), :], buf.at[kk, pl.ds(n, 1), :], sem).start()
        return carry

    lax.fori_loop(0, tm, start, 0)

    def wait(n, carry):
        for kk in range(TOP_K):
            _row_copy(ys_ref.at[pl.ds(0, 1), :], buf.at[0, pl.ds(0, 1), :], sem).wait()
        return carry

    lax.fori_loop(0, tm, wait, 0)

    acc = base_ref[...]
    for kk in range(TOP_K):
        acc = acc + route_ref[:, ROUTE_GATE + kk:ROUTE_GATE + kk + 1] * buf[kk]
    out = _layernorm_rows(acc, g2_ref[...], b2_ref[...])
    xo_ref[...] = out
    xb_ref[...] = out.astype(BF16)


def _combine(dest, base, route, ys, g2, b2):
    n = base.shape[0]
    tm = TOK_TILE
    tok = lambda width: pl.BlockSpec((tm, width), lambda i: (i, 0))
    return pl.pallas_call(
        _combine_kernel,
        grid=(n // tm,),
        in_specs=[pl.BlockSpec((tm * TOP_K,), lambda i: (i,), memory_space=pltpu.SMEM),
                  tok(D_MODEL), tok(LANES), pl.BlockSpec(memory_space=pl.ANY),
                  pl.BlockSpec((1, D_MODEL), lambda i: (0, 0)), pl.BlockSpec((1, D_MODEL), lambda i: (0, 0))],
        out_specs=[tok(D_MODEL), tok(D_MODEL)],
        out_shape=[jax.ShapeDtypeStruct((n, D_MODEL), F32), jax.ShapeDtypeStruct((n, D_MODEL), BF16)],
        scratch_shapes=[pltpu.VMEM((TOP_K, tm, D_MODEL), F32), pltpu.SemaphoreType.DMA(())],
        compiler_params=_cparams(("arbitrary",)),
        name="moe_combine_ln2",
    )(dest, base, route, ys, g2, b2)


def _row(v):
    return v.reshape(1, -1).astype(F32)


def _pad_rows(m, rows, offset):
    out = jnp.zeros((rows, m.shape[1]), m.dtype)
    return lax.dynamic_update_slice(out, m, (offset, 0))


def _regroup_w_in(w_in):
    d = w_in.shape[0]
    za = w_in[:, 0:A_MAIN]
    zb = w_in[:, 1792:3840]
    zc = w_in[:, 3840:5888]
    igfg = w_in[:, 5888:5896]
    zd = w_in[:, 5896:6920]
    zg = w_in[:, 6920:11016]
    vr = w_in[:, 11016:] if w_in.shape[1] > 11016 else jnp.zeros((d, 32), w_in.dtype)
    small = jnp.concatenate([vr, igfg, jnp.zeros((d, LANES - 40), w_in.dtype)], axis=1)
    a_grp = jnp.concatenate([za, small, jnp.zeros((d, 2048 - A_MAIN - LANES), w_in.dtype)], axis=1)
    return jnp.concatenate([zg, zb, zc, a_grp, zd], axis=1).astype(BF16)


def _block_diag(w):
    g, di, dj = w.shape
    eye = jnp.eye(g, dtype=w.dtype)
    return (eye[:, None, :, None] * w[:, :, None, :]).reshape(g * di, g * dj)


def kernel(x, p, w_in_first, w_in_rest, rwkv_mu, rwkv_w0, rwkv_w2, rwkv_a0, rwkv_a2, rwkv_v0, rwkv_v2,
           rwkv_g2, rwkv_k_k, rwkv_k_a, rwkv_r_k, rwkv_lnx_g, rwkv_lnx_b, hgrn_lower_bounds, hgrn_norm_w,
           mlstm_conv_w, mlstm_conv_b, mlstm_i_bias, mlstm_f_bias, mlstm_norm_w, lru_conv_w, lru_conv_b,
           lru_wx, lru_bx, lru_wa, lru_ba, lru_lambda, w_branch, w_out, ln1_g, ln1_b, router_w, router_b,
           expert_w_gu, expert_b_gu, expert_w_down, expert_b_down, ple_gate_w, ple_proj_w, ln2_g, ln2_b):
    bsz, seq, _ = x.shape
    n = bsz * seq
    n_assign = n * TOP_K
    n_blocks = n_assign // EXPERT_ROWS + N_EXPERTS
    n_slots = n_blocks * EXPERT_ROWS

    lb_all = jnp.cumsum(jax.nn.softmax(hgrn_lower_bounds.astype(F32), axis=0), axis=0)
    lb_all = lb_all - lb_all[0]

    xf = x.reshape(n, D_MODEL).astype(F32)
    xb = xf.astype(BF16)
    v_first = jnp.zeros((n, W), F32)
    for layer in range(DEPTH):
        first = layer == 0
        w_in = _regroup_w_in(w_in_first if first else w_in_rest[layer - 1])
        z = _proj(xb, w_in)

        rw_prm = {
            "mu": _row(rwkv_mu[layer]),
            "w0": _row(rwkv_w0[layer]),
            "w2": _pad_rows(rwkv_w2[layer], LANES, 0).astype(BF16),
            "a0": _row(rwkv_a0[layer]),
            "a2": _pad_rows(rwkv_a2[layer], LANES, 64).astype(BF16),
            "v0": _row(jnp.zeros((W,), F32) if first else rwkv_v0[layer - 1]),
            "v2": (jnp.zeros((LANES, W), BF16) if first
                   else _pad_rows(rwkv_v2[layer - 1], LANES, SMALL_VR).astype(BF16)),
            "g2": rwkv_g2[layer].astype(BF16),
            "k_k": _row(rwkv_k_k[layer]),
            "k_a": _row(rwkv_k_a[layer]),
            "r_k": _row(rwkv_r_k[layer]),
            "lnx_g": _row(rwkv_lnx_g[layer]),
            "lnx_b": _row(rwkv_lnx_b[layer]),
        }
        y_a, v_cur = _rwkv(z, v_first, rw_prm, bsz, seq, first)
        if first:
            v_first = v_cur
        y_b = _hgrn(z, _row(lb_all[layer]), _row(hgrn_norm_w[layer]), bsz, seq)
        gate_bias = jnp.zeros((LANES,), F32)
        gate_bias = gate_bias.at[SMALL_IG:SMALL_IG + MLSTM_HEADS].set(mlstm_i_bias[layer])
        gate_bias = gate_bias.at[SMALL_FG:SMALL_FG + MLSTM_HEADS].set(mlstm_f_bias[layer])
        y_c = _mlstm(z, mlstm_conv_w[layer].astype(F32), _row(mlstm_conv_b[layer]), _row(gate_bias),
                     _row(mlstm_norm_w[layer]), bsz, seq)
        y_d = _lru(z, lru_conv_w[layer].astype(F32), _row(lru_conv_b[layer]),
                   _block_diag(lru_wx[layer]).astype(BF16), _row(lru_bx[layer]),
                   _block_diag(lru_wa[layer]).astype(BF16), _row(lru_ba[layer]), _row(lru_lambda[layer]),
                   bsz, seq)

        x1, base = _merge(xf, (y_a, y_b, y_c, y_d), z, w_branch[layer].astype(BF16), w_out[layer].astype(BF16),
                          _row(ln1_g[layer]), _row(ln1_b[layer]), p[layer].reshape(n, D_PLE),
                          ple_gate_w[layer].astype(BF16), ple_proj_w[layer].astype(BF16))

        rw = jnp.zeros((D_MODEL, LANES), F32).at[:, 0:N_EXPERTS].set(router_w[layer])
        rb = jnp.zeros((1, LANES), F32).at[0, 0:N_EXPERTS].set(router_b[layer])
        route, counts_f = _route(x1, rw, rb)

        idx = route[:, ROUTE_IDX:ROUTE_IDX + TOP_K].astype(I32)
        rank = route[:, ROUTE_RANK:ROUTE_RANK + TOP_K].astype(I32)
        counts = counts_f[0, 0:N_EXPERTS].astype(I32)
        padded = (counts + EXPERT_ROWS - 1) // EXPERT_ROWS * EXPERT_ROWS
        pad_end = jnp.cumsum(padded)
        pad_start = pad_end - padded
        dest = (pad_start[idx] + rank).reshape(-1)
        block_e = jnp.minimum(
            jnp.searchsorted(pad_end, jnp.arange(n_blocks, dtype=I32) * EXPERT_ROWS, side="right"),
            N_EXPERTS - 1).astype(I32)
        n_used = (pad_end[-1:] // EXPERT_ROWS).astype(I32)

        xs = _scatter(dest, x1, n_slots)
        ys = _ffn(block_e, n_used, xs, expert_w_gu[layer].astype(BF16),
                  expert_b_gu[layer].reshape(N_EXPERTS, 1, 2 * D_EXPERT).astype(F32),
                  expert_w_down[layer].astype(BF16),
                  expert_b_down[layer].reshape(N_EXPERTS, 1, D_MODEL).astype(F32))
        xf, xb = _combine(dest, base, route, ys, _row(ln2_g[layer]), _row(ln2_b[layer]))
    return xf.reshape(bsz, seq, D_MODEL).astype(x.dtype)
```

```python
import functools
import math

import jax
import jax.numpy as jnp
from jax import lax
from jax.experimental import pallas as pl
from jax.experimental.pallas import tpu as pltpu

F32 = jnp.float32
BF16 = jnp.bfloat16
I32 = jnp.int32

D_MODEL = 1024
DEPTH = 4
W = 512
RWKV_HEAD_DIM = 64
RWKV_LNX_EPS = 64e-5
MLSTM_HEADS = 4
HEAD128 = 128
CHUNK = 64
LRU_C = 8.0
CONV_WIDTH = 4
N_EXPERTS = 32
TOP_K = 4
D_EXPERT = 1024
SWIGLU_LIMIT = 7.0
SWIGLU_ALPHA = 1.702
D_PLE = 256
LN_EPS = 1e-5
NORM_EPS = 1e-6
ALPHA = (2 * DEPTH) ** 0.25

COL_G = 0
COL_B = 4096
COL_C = 6144
COL_A = 8192
COL_D = 10240
N_COLS = 11264
A_MAIN = 1792
SMALL_VR = 0
SMALL_IG = 32
SMALL_FG = 36

EXPERT_ROWS = 256
TOK_TILE = 256
LANES = 128
VMEM_LIMIT = 48 * 1024 * 1024
FFN_VMEM_LIMIT = 56 * 1024 * 1024


def _cparams(sem):
    return pltpu.CompilerParams(dimension_semantics=sem, vmem_limit_bytes=VMEM_LIMIT)


def _mm(a, b):
    return jnp.dot(a.astype(BF16), b.astype(BF16), preferred_element_type=F32)


def _mm_nt(a, b):
    return lax.dot_general(a.astype(BF16), b.astype(BF16), (((1,), (1,)), ((), ())),
                           preferred_element_type=F32)


def _mm_tn(a, b):
    return jnp.dot(a.T.astype(BF16), b.astype(BF16), preferred_element_type=F32)


def _split3(x):
    hi = x.astype(BF16)
    r1 = x - hi.astype(F32)
    mid = r1.astype(BF16)
    lo = (r1 - mid.astype(F32)).astype(BF16)
    return hi, mid, lo


def _mm_exact_lhs(m, x):
    hi, mid, lo = _split3(x)
    return (jnp.dot(m, hi, preferred_element_type=F32) + jnp.dot(m, mid, preferred_element_type=F32)
            + jnp.dot(m, lo, preferred_element_type=F32))


def _mm_exact_rhs(x, m):
    hi, mid, lo = _split3(x)
    return (jnp.dot(hi, m, preferred_element_type=F32) + jnp.dot(mid, m, preferred_element_type=F32)
            + jnp.dot(lo, m, preferred_element_type=F32))


def _iota(shape, dim):
    return lax.broadcasted_iota(I32, shape, dim)


def _tri_incl(n):
    return jnp.where(_iota((n, n), 1) <= _iota((n, n), 0), 1.0, 0.0).astype(BF16)


def _seg_ones(n, seg):
    return jnp.where(_iota((n, n), 0) // seg == _iota((n, n), 1) // seg, 1.0, 0.0).astype(BF16)


def _sigmoid(x):
    return jax.nn.sigmoid(x)


def _silu(x):
    return x * jax.nn.sigmoid(x)


def _log_sigmoid(x):
    return jnp.minimum(x, 0.0) - jnp.log1p(jnp.exp(-jnp.abs(x)))


def _layernorm_rows(x, g, b):
    xc = x - jnp.mean(x, -1, keepdims=True)
    var = jnp.mean(xc * xc, -1, keepdims=True)
    return xc * lax.rsqrt(var + LN_EPS) * g + b


def _proj_kernel(x_ref, w_ref, o_ref):
    o_ref[...] = jnp.dot(x_ref[...], w_ref[...], preferred_element_type=F32)


def _proj(xb, w):
    n, k = xb.shape
    c = w.shape[1]
    tm = min(1024, n)
    tn = 1024
    return pl.pallas_call(
        _proj_kernel,
        grid=(c // tn, n // tm),
        in_specs=[pl.BlockSpec((tm, k), lambda j, i: (i, 0)),
                  pl.BlockSpec((k, tn), lambda j, i: (0, j))],
        out_specs=pl.BlockSpec((tm, tn), lambda j, i: (i, j)),
        out_shape=jax.ShapeDtypeStruct((n, c), F32),
        compiler_params=_cparams(("parallel", "parallel")),
        name="in_proj",
    )(xb, w)


def _rwkv_kernel(first, z_ref, vf_ref, mu_ref, w0_ref, w2_ref, a0_ref, a2_ref, v0_ref, v2_ref, g2_ref,
                 kk_ref, ka_ref, rk_ref, lng_ref, lnb_ref, y_ref, vo_ref, prev_ref, s_ref):
    t_len = CHUNK

    @pl.when(pl.program_id(1) == 0)
    def _():
        prev_ref[...] = jnp.zeros_like(prev_ref)
        s_ref[...] = jnp.zeros_like(s_ref)

    z = z_ref[:, 0:A_MAIN]
    small = z_ref[:, A_MAIN:A_MAIN + LANES]
    row = _iota((t_len, 1), 0)
    zs = jnp.where(row == 0, prev_ref[...], pltpu.roll(z, 1, 0))
    prev_ref[...] = z[t_len - 1:t_len, :]
    zz = z + (zs - z) * mu_ref[...]

    r = zz[:, 0:W]
    k = zz[:, W:2 * W]
    v = zz[:, 2 * W:3 * W]
    xwa = zz[:, 3 * W:3 * W + LANES]
    xg = zz[:, 3 * W + LANES:3 * W + 2 * LANES]

    u = w0_ref[...] + _mm(jnp.tanh(xwa), w2_ref[...])
    logw = -math.exp(-0.5) * _sigmoid(u)
    a_sig = _sigmoid(a0_ref[...] + _mm(xwa, a2_ref[...]))
    g = _mm(_sigmoid(xg), g2_ref[...])
    if not first:
        v = v + (vf_ref[...] - v) * _sigmoid(v0_ref[...] + _mm(small, v2_ref[...]))
    vo_ref[...] = v

    seg = _seg_ones(LANES, RWKV_HEAD_DIM)
    tri = _tri_incl(t_len)
    c = _mm_exact_lhs(tri, logw)
    c_last = c[t_len - 1:t_len, :]
    gam = jnp.exp(c)
    igam = jnp.exp(-c)
    gam_ex = jnp.exp(c - logw)
    tail = jnp.exp(c_last - c)
    g_last = jnp.exp(c_last)

    col64 = _iota((t_len, t_len), 1)
    row64 = _iota((t_len, t_len), 0)
    strict = col64 < row64
    incl = col64 <= row64
    lane = _iota((1, LANES), 1)
    bd_mask = _iota((LANES, LANES), 0) // RWKV_HEAD_DIM == _iota((LANES, LANES), 1) // RWKV_HEAD_DIM

    n_pair = W // LANES
    sls = [slice(p * LANES, (p + 1) * LANES) for p in range(n_pair)]
    hmasks = [lane < RWKV_HEAD_DIM, lane >= RWKV_HEAD_DIM]
    kk_raw = [k[:, sl] * kk_ref[:, sl] for sl in sls]
    ss = [_mm_exact_rhs(kr * kr, seg) for kr in kk_raw]
    kkn = [kr / jnp.maximum(jnp.sqrt(s_), 1e-12) for kr, s_ in zip(kk_raw, ss)]
    aps = [a_sig[:, sl] for sl in sls]
    k2s = [k[:, sl] * (1.0 + (ap - 1.0) * ka_ref[:, sl]) for sl, ap in zip(sls, aps)]
    rps = [r[:, sl] for sl in sls]
    vps = [v[:, sl] for sl in sls]
    r_ts = [rp * gam[:, sl] for rp, sl in zip(rps, sls)]
    a_ts = [-kn * gam_ex[:, sl] for kn, sl in zip(kkn, sls)]
    b_ts = [kn * ap * igam[:, sl] for kn, ap, sl in zip(kkn, aps, sls)]
    k_ts = [k2 * igam[:, sl] for k2, sl in zip(k2s, sls)]
    b_hats = [kn * ap * tail[:, sl] for kn, ap, sl in zip(kkn, aps, sls)]
    k_hats = [k2 * tail[:, sl] for k2, sl in zip(k2s, sls)]
    s_mats = [s_ref[p] for p in range(n_pair)]
    ah0 = [_mm_nt(a_ts[p], s_mats[p]) for p in range(n_pair)]
    rh0 = [_mm_nt(r_ts[p], s_mats[p]) for p in range(n_pair)]

    heads = [(p, hh) for p in range(n_pair) for hh in range(2)]
    lhs = [jnp.concatenate([jnp.where(hmasks[hh], a_ts[p], 0.0), jnp.where(hmasks[hh], r_ts[p], 0.0)], axis=0)
           for p, hh in heads]
    pb = [_mm_nt(lhs[i], b_ts[p]) for i, (p, hh) in enumerate(heads)]
    pk = [_mm_nt(lhs[i], k_ts[p]) for i, (p, hh) in enumerate(heads)]
    a_ak = [jnp.where(strict, m[0:t_len], 0.0) for m in pk]
    a_rb = [jnp.where(incl, m[t_len:], 0.0) for m in pb]
    a_rk = [jnp.where(incl, m[t_len:], 0.0) for m in pk]
    pw = [jnp.where(strict, m[0:t_len], 0.0) for m in pb]
    xs = [ah0[p] + _mm(a_ak[i], vps[p]) for i, (p, hh) in enumerate(heads)]
    for it in range(6):
        xs = [x + _mm(q_, x) for x, q_ in zip(xs, pw)]
        if it < 5:
            pw = [_mm(q_, q_) for q_ in pw]
    yh = [rh0[p] + _mm(a_rb[i], xs[i]) + _mm(a_rk[i], vps[p]) for i, (p, hh) in enumerate(heads)]
    u_pairs = [jnp.where(hmasks[0], xs[2 * p], xs[2 * p + 1]) for p in range(n_pair)]
    y_pairs = [jnp.where(hmasks[0], yh[2 * p], yh[2 * p + 1]) for p in range(n_pair)]

    upd = [_mm_tn(u_pairs[p], b_hats[p]) + _mm_tn(vps[p], k_hats[p]) for p in range(n_pair)]
    for p in range(n_pair):
        s_ref[p] = s_mats[p] * g_last[:, sls[p]] + jnp.where(bd_mask, upd[p], 0.0)

    means = [_mm_exact_rhs(y_, seg) * (1.0 / RWKV_HEAD_DIM) for y_ in y_pairs]
    ycs = [y_ - m_ for y_, m_ in zip(y_pairs, means)]
    vars_ = [_mm_exact_rhs(yc * yc, seg) * (1.0 / RWKV_HEAD_DIM) for yc in ycs]
    bonus = [_mm_exact_rhs(rps[p] * k2s[p] * rk_ref[:, sls[p]], seg) * vps[p] for p in range(n_pair)]
    for p in range(n_pair):
        sl = sls[p]
        yn = ycs[p] * lax.rsqrt(vars_[p] + RWKV_LNX_EPS) * lng_ref[:, sl] + lnb_ref[:, sl]
        y_ref[:, sl] = ((yn + bonus[p]) * g[:, sl]).astype(y_ref.dtype)


def _rwkv(z, v_first, prm, bsz, seq, first):
    n = bsz * seq
    nj = seq // CHUNK
    row_spec = lambda width: pl.BlockSpec((1, width), lambda b, j: (0, 0))
    mat_spec = lambda r_, c_: pl.BlockSpec((r_, c_), lambda b, j: (0, 0))
    tok = lambda width: pl.BlockSpec((CHUNK, width), lambda b, j: (b * nj + j, 0))
    in_specs = [
        pl.BlockSpec((CHUNK, 2048), lambda b, j: (b * nj + j, COL_A // 2048)),
        tok(W),
        row_spec(A_MAIN), row_spec(W), mat_spec(LANES, W), row_spec(W), mat_spec(LANES, W),
        row_spec(W), mat_spec(LANES, W), mat_spec(LANES, W),
        row_spec(W), row_spec(W), row_spec(W), row_spec(W), row_spec(W),
    ]
    return pl.pallas_call(
        functools.partial(_rwkv_kernel, first),
        grid=(bsz, nj),
        in_specs=in_specs,
        out_specs=[tok(W), tok(W)],
        out_shape=[jax.ShapeDtypeStruct((n, W), BF16), jax.ShapeDtypeStruct((n, W), F32)],
        scratch_shapes=[pltpu.VMEM((1, A_MAIN), F32), pltpu.VMEM((W // LANES, LANES, LANES), F32)],
        compiler_params=_cparams(("parallel", "arbitrary")),
        name="rwkv7",
    )(z, v_first, prm["mu"], prm["w0"], prm["w2"], prm["a0"], prm["a2"], prm["v0"], prm["v2"], prm["g2"],
      prm["k_k"], prm["k_a"], prm["r_k"], prm["lnx_g"], prm["lnx_b"])


HG_SUB = 16


def _hgrn_kernel(z_ref, lb_ref, nw_ref, y_ref, s_ref, q_s, k_s, v_s, bc_s, o_s):
    t_len = CHUNK

    @pl.when(pl.program_id(1) == 0)
    def _():
        s_ref[...] = jnp.zeros_like(s_ref)

    zq = z_ref[:, 0:W]
    f = z_ref[:, W:2 * W]
    lb = lb_ref[...]
    q_s[...] = _silu(zq)
    v_s[...] = z_ref[:, 2 * W:3 * W]
    la = jnp.log(lb)
    lbb = jnp.log1p(-lb) + _log_sigmoid(f)
    logf = jnp.maximum(la, lbb) + jnp.log1p(jnp.exp(-jnp.abs(la - lbb)))
    k_s[...] = (1.0 - lb) * _sigmoid(-f)
    blk = jnp.where((_iota((t_len, t_len), 1) <= _iota((t_len, t_len), 0))
                    & (_iota((t_len, t_len), 1) // HG_SUB == _iota((t_len, t_len), 0) // HG_SUB),
                    1.0, 0.0).astype(BF16)
    bc_s[...] = _mm_exact_lhs(blk, logf)

    row16 = _iota((HG_SUB, 1), 0)

    def sub_block(i, carry):
        r0 = pl.multiple_of(i * HG_SUB, HG_SUB)
        n_head = W // HEAD128
        sls = [slice(h * HEAD128, (h + 1) * HEAD128) for h in range(n_head)]
        qb = q_s[pl.ds(r0, HG_SUB), :]
        kb = k_s[pl.ds(r0, HG_SUB), :]
        vb = v_s[pl.ds(r0, HG_SUB), :]
        bcb = bc_s[pl.ds(r0, HG_SUB), :]
        st = [s_ref[h] for h in range(n_head)]
        qd = qb * jnp.exp(bcb)
        o = [_mm_nt(qd[:, sl], st[h]) for h, sl in enumerate(sls)]
        for s in range(HG_SUB):
            d = jnp.where(row16 >= s, bcb - bcb[s:s + 1, :], -jnp.inf)
            prod = qb * kb[s:s + 1, :] * jnp.exp(d)
            for h, sl in enumerate(sls):
                o[h] = o[h] + jnp.sum(prod[:, sl], axis=-1, keepdims=True) * vb[s:s + 1, sl]
        bl = bcb[HG_SUB - 1:HG_SUB, :]
        kd = kb * jnp.exp(bl - bcb)
        e_last = jnp.exp(bl)
        upd = [_mm_tn(vb[:, sl], kd[:, sl]) for sl in sls]
        for h, sl in enumerate(sls):
            s_ref[h] = st[h] * e_last[:, sl] + upd[h]
            o_s[pl.ds(r0, HG_SUB), sl] = o[h]
        return carry

    lax.fori_loop(0, t_len // HG_SUB, sub_block, 0)

    g = z_ref[:, 3 * W:4 * W]
    for h in range(W // HEAD128):
        sl = slice(h * HEAD128, (h + 1) * HEAD128)
        o = o_s[:, sl]
        o = o * lax.rsqrt(jnp.mean(o * o, -1, keepdims=True) + NORM_EPS) * nw_ref[:, sl]
        y_ref[:, sl] = (o * _silu(g[:, sl])).astype(y_ref.dtype)


def _hgrn(z, lb, norm_w, bsz, seq):
    n = bsz * seq
    nj = seq // CHUNK
    row_spec = pl.BlockSpec((1, W), lambda b, j: (0, 0))
    return pl.pallas_call(
        _hgrn_kernel,
        grid=(bsz, nj),
        in_specs=[pl.BlockSpec((CHUNK, 2048), lambda b, j: (b * nj + j, COL_B // 2048)), row_spec, row_spec],
        out_specs=pl.BlockSpec((CHUNK, W), lambda b, j: (b * nj + j, 0)),
        out_shape=jax.ShapeDtypeStruct((n, W), BF16),
        scratch_shapes=[pltpu.VMEM((W // HEAD128, HEAD128, HEAD128), F32)]
        + [pltpu.VMEM((CHUNK, W), F32) for _ in range(5)],
        compiler_params=_cparams(("parallel", "arbitrary")),
        name="hgrn2",
    )(z, lb, norm_w)


def _mlstm_kernel(z_ref, sm_ref, cw_ref, cb_ref, gb_ref, nw_ref, y_ref, ext_ref, c_ref, n_ref, m_ref):
    t_len = CHUNK
    pad = 8

    @pl.when(pl.program_id(1) == 0)
    def _():
        ext_ref[0:pad, :] = jnp.zeros((pad, 2 * W), F32)
        c_ref[...] = jnp.zeros_like(c_ref)
        n_ref[...] = jnp.zeros_like(n_ref)
        m_ref[...] = jnp.full(m_ref.shape, -1e30, F32)

    ext_ref[pad:pad + t_len, :] = z_ref[:, 0:2 * W]
    acc = cb_ref[...]
    for jj in range(CONV_WIDTH):
        acc = acc + ext_ref[pl.ds(pad - (CONV_WIDTH - 1) + jj, t_len), :] * cw_ref[jj:jj + 1, :]
    ext_ref[0:pad, :] = z_ref[t_len - pad:t_len, 0:2 * W]
    qk = _silu(acc)
    q = qk[:, 0:W]
    k = qk[:, W:2 * W] * (HEAD128 ** -0.5)
    v = z_ref[:, 2 * W:3 * W]
    og = z_ref[:, 3 * W:4 * W]

    gates = sm_ref[...] + gb_ref[...]
    lf = _log_sigmoid(gates)
    fc = _mm_exact_lhs(_tri_incl(t_len), lf)
    gates_t = gates.T
    fc_t = fc.T
    causal = _iota((t_len, t_len), 1) <= _iota((t_len, t_len), 0)

    hs = range(MLSTM_HEADS)
    sls = [slice(h * HEAD128, (h + 1) * HEAD128) for h in hs]
    ic_col = [gates[:, SMALL_IG + h:SMALL_IG + h + 1] for h in hs]
    ic_row = [gates_t[SMALL_IG + h:SMALL_IG + h + 1, :] for h in hs]
    fc_col = [fc[:, SMALL_FG + h:SMALL_FG + h + 1] for h in hs]
    fc_row = [fc_t[SMALL_FG + h:SMALL_FG + h + 1, :] for h in hs]
    m_prev = [m_ref[h:h + 1, 0:1] for h in hs]
    c_mat = [c_ref[h] for h in hs]
    n_vec = [n_ref[h:h + 1, :] for h in hs]
    qk_raw = [_mm_nt(q[:, sl], k[:, sl]) for sl in sls]
    q_c = [_mm(q[:, sl], c_mat[h]) for h, sl in enumerate(sls)]
    log_w = [jnp.where(causal, fc_col[h] - fc_row[h] + ic_row[h], -jnp.inf) for h in hs]
    log_inter = [fc_col[h] + m_prev[h] for h in hs]
    m_t = [jnp.maximum(log_inter[h], jnp.max(log_w[h], -1, keepdims=True)) for h in hs]
    w_inter = [jnp.exp(log_inter[h] - m_t[h]) for h in hs]
    s_qk = [qk_raw[h] * jnp.exp(log_w[h] - m_t[h]) for h in hs]
    s_v = [_mm(s_qk[h], v[:, sl]) for h, sl in enumerate(sls)]
    f_last = [fc_col[h][t_len - 1:t_len, :] for h in hs]
    log_s = [f_last[h] - fc_col[h] + ic_col[h] for h in hs]
    m_new = [jnp.maximum(f_last[h] + m_prev[h], jnp.max(log_s[h], 0, keepdims=True)) for h in hs]
    kw = [k[:, sl] * jnp.exp(log_s[h] - m_new[h]) for h, sl in enumerate(sls)]
    kv = [_mm_tn(kw[h], v[:, sl]) for h, sl in enumerate(sls)]
    for h, sl in enumerate(sls):
        num = w_inter[h] * q_c[h] + s_v[h]
        den = (w_inter[h] * jnp.sum(q[:, sl] * n_vec[h], -1, keepdims=True)
               + jnp.sum(s_qk[h], -1, keepdims=True))
        hid = num / jnp.maximum(jnp.abs(den), jnp.exp(-m_t[h]))
        decay = jnp.exp(f_last[h] + m_prev[h] - m_new[h])
        c_ref[h] = decay * c_mat[h] + kv[h]
        n_ref[h:h + 1, :] = decay * n_vec[h] + jnp.sum(kw[h], 0, keepdims=True)
        m_ref[h:h + 1, :] = jnp.broadcast_to(m_new[h], (1, LANES))
        hc = hid - jnp.mean(hid, -1, keepdims=True)
        hn = hc * lax.rsqrt(jnp.mean(hc * hc, -1, keepdims=True) + NORM_EPS) * nw_ref[:, sl]
        y_ref[:, sl] = (_sigmoid(og[:, sl]) * hn).astype(y_ref.dtype)


def _mlstm(z, conv_w, conv_b, gate_bias, norm_w, bsz, seq):
    n = bsz * seq
    nj = seq // CHUNK
    const = lambda r_, c_: pl.BlockSpec((r_, c_), lambda b, j: (0, 0))
    return pl.pallas_call(
        _mlstm_kernel,
        grid=(bsz, nj),
        in_specs=[pl.BlockSpec((CHUNK, 2048), lambda b, j: (b * nj + j, COL_C // 2048)),
                  pl.BlockSpec((CHUNK, LANES), lambda b, j: (b * nj + j, (COL_A + A_MAIN) // LANES)),
                  const(CONV_WIDTH, 2 * W), const(1, 2 * W), const(1, LANES), const(1, W)],
        out_specs=pl.BlockSpec((CHUNK, W), lambda b, j: (b * nj + j, 0)),
        out_shape=jax.ShapeDtypeStruct((n, W), BF16),
        scratch_shapes=[pltpu.VMEM((CHUNK + 8, 2 * W), F32),
                        pltpu.VMEM((MLSTM_HEADS, HEAD128, HEAD128), F32),
                        pltpu.VMEM((8, HEAD128), F32), pltpu.VMEM((8, LANES), F32)],
        compiler_params=_cparams(("parallel", "arbitrary")),
        name="mlstm",
    )(z, z, conv_w, conv_b, gate_bias, norm_w)


LRU_T = 256


def _lru_kernel(z_ref, cw_ref, cb_ref, wx_ref, bx_ref, wa_ref, ba_ref, lam_ref, y_ref, ext_ref, h_ref):
    t_len = LRU_T
    pad = 8
    first_blk = pl.program_id(1) == 0

    @pl.when(first_blk)
    def _():
        ext_ref[0:pad, :] = jnp.zeros((pad, W), F32)
        h_ref[...] = jnp.zeros_like(h_ref)

    ext_ref[pad:pad + t_len, :] = z_ref[:, 0:W]
    xc = cb_ref[...]
    for jj in range(CONV_WIDTH):
        xc = xc + ext_ref[pl.ds(pad - (CONV_WIDTH - 1) + jj, t_len), :] * cw_ref[jj:jj + 1, :]
    ext_ref[0:pad, :] = z_ref[t_len - pad:t_len, 0:W]

    gate_x = _sigmoid(_mm(xc, wx_ref[...]) + bx_ref[...])
    gate_a = _sigmoid(_mm(xc, wa_ref[...]) + ba_ref[...])
    log_a = LRU_C * gate_a * _log_sigmoid(lam_ref[...])
    a = jnp.exp(log_a)
    mult = jnp.sqrt(-jnp.tanh(log_a) * (a * a + 1.0))
    row = _iota((t_len, 1), 0)
    mult = jnp.where(jnp.logical_and(first_blk, row == 0), 1.0, mult)
    b = mult * gate_x * xc

    d = 1
    while d < t_len:
        keep = row >= d
        a_sh = jnp.where(keep, pltpu.roll(a, d, 0), 1.0)
        b_sh = jnp.where(keep, pltpu.roll(b, d, 0), 0.0)
        b = a * b_sh + b
        a = a * a_sh
        d *= 2
    h = a * h_ref[...] + b
    h_ref[...] = h[t_len - 1:t_len, :]
    y_ref[...] = (h * jax.nn.gelu(z_ref[:, W:2 * W], approximate=True)).astype(y_ref.dtype)


def _lru(z, conv_w, conv_b, wx, bx, wa, ba, lam, bsz, seq):
    n = bsz * seq
    nj = seq // LRU_T
    const = lambda r_, c_: pl.BlockSpec((r_, c_), lambda b, j: (0, 0))
    return pl.pallas_call(
        _lru_kernel,
        grid=(bsz, nj),
        in_specs=[pl.BlockSpec((LRU_T, 2 * W), lambda b, j: (b * nj + j, COL_D // (2 * W))),
                  const(CONV_WIDTH, W), const(1, W), const(W, W), const(1, W), const(W, W), const(1, W),
                  const(1, W)],
        out_specs=pl.BlockSpec((LRU_T, W), lambda b, j: (b * nj + j, 0)),
        out_shape=jax.ShapeDtypeStruct((n, W), BF16),
        scratch_shapes=[pltpu.VMEM((LRU_T + 8, W), F32), pltpu.VMEM((1, W), F32)],
        compiler_params=_cparams(("parallel", "arbitrary")),
        name="rglru",
    )(z, conv_w, conv_b, wx, bx, wa, ba, lam)


def _merge_kernel(x_ref, ya_ref, yb_ref, yc_ref, yd_ref, zg_ref, wb_ref, wo_ref, g1_ref, b1_ref,
                  p_ref, wpg_ref, wpp_ref, x1_ref, base_ref):
    merged = None
    for nb, y_ref in enumerate((ya_ref, yb_ref, yc_ref, yd_ref)):
        proj = jnp.dot(y_ref[...], wb_ref[nb], preferred_element_type=F32)
        term = _sigmoid(zg_ref[:, nb * D_MODEL:(nb + 1) * D_MODEL]) * proj
        merged = term if merged is None else merged + term
    mix = _mm(merged, wo_ref[...])
    x1 = _layernorm_rows(ALPHA * x_ref[...] + mix, g1_ref[...], b1_ref[...])
    x1_ref[...] = x1
    ple = _sigmoid(_mm(x1, wpg_ref[...])) * _mm(p_ref[...], wpp_ref[...])
    base_ref[...] = ALPHA * x1 + ple


def _merge(x, ys, z, w_branch, w_out, g1, b1, p_l, w_pg, w_pp):
    n = x.shape[0]
    tm = TOK_TILE
    tok = lambda width: pl.BlockSpec((tm, width), lambda i: (i, 0))
    const2 = lambda r_, c_: pl.BlockSpec((r_, c_), lambda i: (0, 0))
    return pl.pallas_call(
        _merge_kernel,
        grid=(n // tm,),
        in_specs=[tok(D_MODEL), tok(W), tok(W), tok(W), tok(W),
                  pl.BlockSpec((tm, 4 * D_MODEL), lambda i: (i, COL_G // (4 * D_MODEL))),
                  pl.BlockSpec((4, W, D_MODEL), lambda i: (0, 0, 0)), const2(D_MODEL, D_MODEL),
                  const2(1, D_MODEL), const2(1, D_MODEL), tok(D_PLE), const2(D_MODEL, D_MODEL),
                  const2(D_PLE, D_MODEL)],
        out_specs=[tok(D_MODEL), tok(D_MODEL)],
        out_shape=[jax.ShapeDtypeStruct((n, D_MODEL), F32), jax.ShapeDtypeStruct((n, D_MODEL), F32)],
        compiler_params=_cparams(("parallel",)),
        name="merge_ln1",
    )(x, *ys, z, w_branch, w_out, g1, b1, p_l, w_pg, w_pp)


ROUTE_IDX = 0
ROUTE_RANK = 4
ROUTE_GATE = 8


def _route_kernel(x_ref, rw_ref, rb_ref, out_ref, cnt_ref, carry_ref):
    tm = x_ref.shape[0]

    @pl.when(pl.program_id(0) == 0)
    def _():
        carry_ref[...] = jnp.zeros_like(carry_ref)

    xh, xm, _ = _split3(x_ref[...])
    wh, wm, _ = _split3(rw_ref[...])
    logits = (jnp.dot(xh, wh, preferred_element_type=F32) + jnp.dot(xh, wm, preferred_element_type=F32)
              + jnp.dot(xm, wh, preferred_element_type=F32)) + rb_ref[...]
    lane = _iota((tm, LANES), 1)
    lane_f = lane.astype(F32)
    cur = jnp.where(lane < N_EXPERTS, logits, -jnp.inf)
    vals, idxs, hots = [], [], []
    for _k in range(TOP_K):
        m = jnp.max(cur, -1, keepdims=True)
        ik = jnp.min(jnp.where(cur == m, lane_f, float(LANES)), -1, keepdims=True)
        hot = lane_f == ik
        vals.append(m)
        idxs.append(ik)
        hots.append(hot)
        cur = jnp.where(hot, -jnp.inf, cur)
    exps = [jnp.exp(vk - vals[0]) for vk in vals]
    denom = exps[0] + exps[1] + exps[2] + exps[3]
    count = jnp.zeros((tm, LANES), F32)
    for hot in hots:
        count = count + jnp.where(hot, 1.0, 0.0)
    strict = jnp.where(_iota((tm, tm), 1) < _iota((tm, tm), 0), 1.0, 0.0).astype(BF16)
    before = jnp.dot(strict, count.astype(BF16), preferred_element_type=F32) + carry_ref[...]
    out = jnp.zeros((tm, LANES), F32)
    for kk in range(TOP_K):
        rank = jnp.sum(jnp.where(hots[kk], before, 0.0), -1, keepdims=True)
        out = jnp.where(lane == ROUTE_IDX + kk, idxs[kk], out)
        out = jnp.where(lane == ROUTE_RANK + kk, rank, out)
        out = jnp.where(lane == ROUTE_GATE + kk, exps[kk] / denom, out)
    out_ref[...] = out
    total = carry_ref[...] + jnp.sum(count, 0, keepdims=True)
    carry_ref[...] = total
    cnt_ref[...] = total


def _route(x1, rw, rb):
    n = x1.shape[0]
    tm = TOK_TILE
    return pl.pallas_call(
        _route_kernel,
        grid=(n // tm,),
        in_specs=[pl.BlockSpec((tm, D_MODEL), lambda i: (i, 0)),
                  pl.BlockSpec((D_MODEL, LANES), lambda i: (0, 0)),
                  pl.BlockSpec((1, LANES), lambda i: (0, 0))],
        out_specs=[pl.BlockSpec((tm, LANES), lambda i: (i, 0)), pl.BlockSpec((1, LANES), lambda i: (0, 0))],
        out_shape=[jax.ShapeDtypeStruct((n, LANES), F32), jax.ShapeDtypeStruct((1, LANES), F32)],
        scratch_shapes=[pltpu.VMEM((1, LANES), F32)],
        compiler_params=_cparams(("arbitrary",)),
        name="router",
    )(x1, rw, rb)


def _row_copy(src, dst, sem):
    return pltpu.make_async_copy(src, dst, sem)


def _scatter_kernel(dest_ref, x_ref, xs_in_ref, xs_ref, sem):
    del xs_in_ref
    tm = x_ref.shape[0]

    def start(n, carry):
        for kk in range(TOP_K):
            d = dest_ref[n * TOP_K + kk]
            _row_copy(x_ref.at[pl.ds(n, 1), :], xs_ref.at[pl.ds(d, 1), :], sem).start(priority=kk % 2)
        return carry

    lax.fori_loop(0, tm, start, 0)

    def wait(n, carry):
        for kk in range(TOP_K):
            _row_copy(x_ref.at[pl.ds(0, 1), :], xs_ref.at[pl.ds(0, 1), :], sem).wait()
        return carry

    lax.fori_loop(0, tm, wait, 0)


def _scatter(dest, x1, n_slots):
    n = x1.shape[0]
    tm = TOK_TILE
    zeros = jnp.zeros((n_slots, D_MODEL), F32)
    return pl.pallas_call(
        _scatter_kernel,
        grid=(n // tm,),
        in_specs=[pl.BlockSpec((tm * TOP_K,), lambda i: (i,), memory_space=pltpu.SMEM),
                  pl.BlockSpec((tm, D_MODEL), lambda i: (i, 0)),
                  pl.BlockSpec(memory_space=pl.ANY)],
        out_specs=pl.BlockSpec(memory_space=pl.ANY),
        out_shape=jax.ShapeDtypeStruct((n_slots, D_MODEL), F32),
        scratch_shapes=[pltpu.SemaphoreType.DMA(())],
        input_output_aliases={2: 0},
        compiler_params=_cparams(("arbitrary",)),
        name="moe_scatter",
    )(dest, x1, zeros)


def _ffn_kernel(be_ref, nu_ref, xs_ref, wgu_ref, bgu_ref, wd_ref, bd_ref, ys_ref, wgu_b, wd_b):
    i = pl.program_id(0)
    live = i < nu_ref[0]
    fresh = jnp.logical_or(i == 0, be_ref[i] != be_ref[jnp.maximum(i - 1, 0)])

    @pl.when(jnp.logical_and(live, fresh))
    def _():
        wgu_b[...] = wgu_ref[0, 0].astype(BF16)
        wd_b[...] = wd_ref[0, 0].astype(BF16)

    @pl.when(live)
    def _():
        gu = jnp.dot(xs_ref[...].astype(BF16), wgu_b[...], preferred_element_type=F32) + bgu_ref[0, 0]
        gate = jnp.minimum(gu[:, 0:D_EXPERT], SWIGLU_LIMIT)
        up = jnp.clip(gu[:, D_EXPERT:], -SWIGLU_LIMIT, SWIGLU_LIMIT)
        glu = gate * _sigmoid(SWIGLU_ALPHA * gate)
        act = ((up + 1.0) * glu).astype(BF16)
        ys_ref[...] = jnp.dot(act, wd_b[...], preferred_element_type=F32) + bd_ref[0, 0]

    @pl.when(jnp.logical_not(live))
    def _():
        ys_ref[...] = jnp.zeros_like(ys_ref)


def _ffn(layer, block_e, n_used, xs, w_gu, b_gu, w_down, b_down):
    n_slots = xs.shape[0]
    bm = EXPERT_ROWS
    grid_spec = pltpu.PrefetchScalarGridSpec(
        num_scalar_prefetch=2,
        grid=(n_slots // bm,),
        in_specs=[pl.BlockSpec((bm, D_MODEL), lambda i, be, nu: (i, 0)),
                  pl.BlockSpec((1, 1, D_MODEL, 2 * D_EXPERT), lambda i, be, nu: (layer, be[i], 0, 0)),
                  pl.BlockSpec((1, 1, 1, 2 * D_EXPERT), lambda i, be, nu: (layer, be[i], 0, 0)),
                  pl.BlockSpec((1, 1, D_EXPERT, D_MODEL), lambda i, be, nu: (layer, be[i], 0, 0)),
                  pl.BlockSpec((1, 1, 1, D_MODEL), lambda i, be, nu: (layer, be[i], 0, 0))],
        out_specs=pl.BlockSpec((bm, D_MODEL), lambda i, be, nu: (i, 0)),
        scratch_shapes=[pltpu.VMEM((D_MODEL, 2 * D_EXPERT), BF16), pltpu.VMEM((D_EXPERT, D_MODEL), BF16)],
    )
    return pl.pallas_call(
        _ffn_kernel,
        grid_spec=grid_spec,
        out_shape=jax.ShapeDtypeStruct((n_slots, D_MODEL), F32),
        compiler_params=pltpu.CompilerParams(dimension_semantics=("arbitrary",),
                                             vmem_limit_bytes=FFN_VMEM_LIMIT),
        name="moe_ffn",
    )(block_e, n_used, xs, w_gu, b_gu, w_down, b_down)


def _combine_kernel(dest_ref, base_ref, route_ref, ys_ref, g2_ref, b2_ref, xo_ref, xb_ref, buf, sem):
    tm = base_ref.shape[0]

    def start(n, carry):
        for kk in range(TOP_K):
            d = dest_ref[n * TOP_K + kk]
            _row_copy(ys_ref.at[pl.ds(d, 1), :], buf.at[kk, pl.ds(n, 1), :], sem).start(priority=kk % 2)
        return carry

    lax.fori_loop(0, tm, start, 0)

    def wait(n, carry):
        for kk in range(TOP_K):
            _row_copy(ys_ref.at[pl.ds(0, 1), :], buf.at[0, pl.ds(0, 1), :], sem).wait()
        return carry

    lax.fori_loop(0, tm, wait, 0)

    acc = base_ref[...]
    for kk in range(TOP_K):
        acc = acc + route_ref[:, ROUTE_GATE + kk:ROUTE_GATE + kk + 1] * buf[kk]
    out = _layernorm_rows(acc, g2_ref[...], b2_ref[...])
    xo_ref[...] = out
    xb_ref[...] = out.astype(BF16)


def _combine(dest, base, route, ys, g2, b2):
    n = base.shape[0]
    tm = TOK_TILE
    tok = lambda width: pl.BlockSpec((tm, width), lambda i: (i, 0))
    return pl.pallas_call(
        _combine_kernel,
        grid=(n // tm,),
        in_specs=[pl.BlockSpec((tm * TOP_K,), lambda i: (i,), memory_space=pltpu.SMEM),
                  tok(D_MODEL), tok(LANES), pl.BlockSpec(memory_space=pl.ANY),
                  pl.BlockSpec((1, D_MODEL), lambda i: (0, 0)), pl.BlockSpec((1, D_MODEL), lambda i: (0, 0))],
        out_specs=[tok(D_MODEL), tok(D_MODEL)],
        out_shape=[jax.ShapeDtypeStruct((n, D_MODEL), F32), jax.ShapeDtypeStruct((n, D_MODEL), BF16)],
        scratch_shapes=[pltpu.VMEM((TOP_K, tm, D_MODEL), F32), pltpu.SemaphoreType.DMA(())],
        compiler_params=_cparams(("arbitrary",)),
        name="moe_combine_ln2",
    )(dest, base, route, ys, g2, b2)


def _row(v):
    return v.reshape(1, -1).astype(F32)


def _pad_rows(m, rows, offset):
    out = jnp.zeros((rows, m.shape[1]), m.dtype)
    return lax.dynamic_update_slice(out, m, (offset, 0))


def _regroup_w_in(w_in):
    d = w_in.shape[0]
    za = w_in[:, 0:A_MAIN]
    zb = w_in[:, 1792:3840]
    zc = w_in[:, 3840:5888]
    igfg = w_in[:, 5888:5896]
    zd = w_in[:, 5896:6920]
    zg = w_in[:, 6920:11016]
    vr = w_in[:, 11016:] if w_in.shape[1] > 11016 else jnp.zeros((d, 32), w_in.dtype)
    small = jnp.concatenate([vr, igfg, jnp.zeros((d, LANES - 40), w_in.dtype)], axis=1)
    a_grp = jnp.concatenate([za, small, jnp.zeros((d, 2048 - A_MAIN - LANES), w_in.dtype)], axis=1)
    return jnp.concatenate([zg, zb, zc, a_grp, zd], axis=1).astype(BF16)


def _block_diag(w):
    g, di, dj = w.shape
    eye = jnp.eye(g, dtype=w.dtype)
    return (eye[:, None, :, None] * w[:, :, None, :]).reshape(g * di, g * dj)


def kernel(x, p, w_in_first, w_in_rest, rwkv_mu, rwkv_w0, rwkv_w2, rwkv_a0, rwkv_a2, rwkv_v0, rwkv_v2,
           rwkv_g2, rwkv_k_k, rwkv_k_a, rwkv_r_k, rwkv_lnx_g, rwkv_lnx_b, hgrn_lower_bounds, hgrn_norm_w,
           mlstm_conv_w, mlstm_conv_b, mlstm_i_bias, mlstm_f_bias, mlstm_norm_w, lru_conv_w, lru_conv_b,
           lru_wx, lru_bx, lru_wa, lru_ba, lru_lambda, w_branch, w_out, ln1_g, ln1_b, router_w, router_b,
           expert_w_gu, expert_b_gu, expert_w_down, expert_b_down, ple_gate_w, ple_proj_w, ln2_g, ln2_b):
    bsz, seq, _ = x.shape
    n = bsz * seq
    n_assign = n * TOP_K
    n_blocks = n_assign // EXPERT_ROWS + N_EXPERTS
    n_slots = n_blocks * EXPERT_ROWS

    lb_all = jnp.cumsum(jax.nn.softmax(hgrn_lower_bounds.astype(F32), axis=0), axis=0)
    lb_all = lb_all - lb_all[0]

    xf = x.reshape(n, D_MODEL).astype(F32)
    xb = xf.astype(BF16)
    v_first = jnp.zeros((n, W), F32)
    for layer in range(DEPTH):
        first = layer == 0
        w_in = _regroup_w_in(w_in_first if first else w_in_rest[layer - 1])
        z = _proj(xb, w_in)

        rw_prm = {
            "mu": _row(rwkv_mu[layer]),
            "w0": _row(rwkv_w0[layer]),
            "w2": _pad_rows(rwkv_w2[layer], LANES, 0).astype(BF16),
            "a0": _row(rwkv_a0[layer]),
            "a2": _pad_rows(rwkv_a2[layer], LANES, 64).astype(BF16),
            "v0": _row(jnp.zeros((W,), F32) if first else rwkv_v0[layer - 1]),
            "v2": (jnp.zeros((LANES, W), BF16) if first
                   else _pad_rows(rwkv_v2[layer - 1], LANES, SMALL_VR).astype(BF16)),
            "g2": rwkv_g2[layer].astype(BF16),
            "k_k": _row(rwkv_k_k[layer]),
            "k_a": _row(rwkv_k_a[layer]),
            "r_k": _row(rwkv_r_k[layer]),
            "lnx_g": _row(rwkv_lnx_g[layer]),
            "lnx_b": _row(rwkv_lnx_b[layer]),
        }
        y_a, v_cur = _rwkv(z, v_first, rw_prm, bsz, seq, first)
        if first:
            v_first = v_cur
        y_b = _hgrn(z, _row(lb_all[layer]), _row(hgrn_norm_w[layer]), bsz, seq)
        gate_bias = jnp.zeros((LANES,), F32)
        gate_bias = gate_bias.at[SMALL_IG:SMALL_IG + MLSTM_HEADS].set(mlstm_i_bias[layer])
        gate_bias = gate_bias.at[SMALL_FG:SMALL_FG + MLSTM_HEADS].set(mlstm_f_bias[layer])
        y_c = _mlstm(z, mlstm_conv_w[layer].astype(F32), _row(mlstm_conv_b[layer]), _row(gate_bias),
                     _row(mlstm_norm_w[layer]), bsz, seq)
        y_d = _lru(z, lru_conv_w[layer].astype(F32), _row(lru_conv_b[layer]),
                   _block_diag(lru_wx[layer]).astype(BF16), _row(lru_bx[layer]),
                   _block_diag(lru_wa[layer]).astype(BF16), _row(lru_ba[layer]), _row(lru_lambda[layer]),
                   bsz, seq)

        x1, base = _merge(xf, (y_a, y_b, y_c, y_d), z, w_branch[layer].astype(BF16), w_out[layer].astype(BF16),
                          _row(ln1_g[layer]), _row(ln1_b[layer]), p[layer].reshape(n, D_PLE),
                          ple_gate_w[layer].astype(BF16), ple_proj_w[layer].astype(BF16))

        rw = jnp.zeros((D_MODEL, LANES), F32).at[:, 0:N_EXPERTS].set(router_w[layer])
        rb = jnp.zeros((1, LANES), F32).at[0, 0:N_EXPERTS].set(router_b[layer])
        route, counts_f = _route(x1, rw, rb)

        idx = route[:, ROUTE_IDX:ROUTE_IDX + TOP_K].astype(I32)
        rank = route[:, ROUTE_RANK:ROUTE_RANK + TOP_K].astype(I32)
        counts = counts_f[0, 0:N_EXPERTS].astype(I32)
        padded = (counts + EXPERT_ROWS - 1) // EXPERT_ROWS * EXPERT_ROWS
        pad_end = jnp.cumsum(padded)
        pad_start = pad_end - padded
        dest = (pad_start[idx] + rank).reshape(-1)
        block_start = jnp.arange(n_blocks, dtype=I32) * EXPERT_ROWS
        block_e = jnp.minimum(jnp.sum((pad_end[None, :] <= block_start[:, None]).astype(I32), axis=1),
                              N_EXPERTS - 1)
        n_used = (pad_end[-1:] // EXPERT_ROWS).astype(I32)

        xs = _scatter(dest, x1, n_slots)
        ys = _ffn(layer, block_e, n_used, xs, expert_w_gu.astype(F32),
                  expert_b_gu.reshape(DEPTH, N_EXPERTS, 1, 2 * D_EXPERT).astype(F32),
                  expert_w_down.astype(F32),
                  expert_b_down.reshape(DEPTH, N_EXPERTS, 1, D_MODEL).astype(F32))
        xf, xb = _combine(dest, base, route, ys, _row(ln2_g[layer]), _row(ln2_b[layer]))
    return xf.reshape(bsz, seq, D_MODEL).astype(x.dtype)
```

```python
import functools
import math

import jax
import jax.numpy as jnp
from jax import lax
from jax.experimental import pallas as pl
from jax.experimental.pallas import tpu as pltpu

F32 = jnp.float32
BF16 = jnp.bfloat16
I32 = jnp.int32

D_MODEL = 1024
DEPTH = 4
W = 512
RWKV_HEAD_DIM = 64
RWKV_LNX_EPS = 64e-5
MLSTM_HEADS = 4
HEAD128 = 128
CHUNK = 64
LRU_C = 8.0
CONV_WIDTH = 4
N_EXPERTS = 32
TOP_K = 4
D_EXPERT = 1024
SWIGLU_LIMIT = 7.0
SWIGLU_ALPHA = 1.702
D_PLE = 256
LN_EPS = 1e-5
NORM_EPS = 1e-6
ALPHA = (2 * DEPTH) ** 0.25

COL_G = 0
COL_B = 4096
COL_C = 6144
COL_A = 8192
COL_D = 10240
N_COLS = 11264
A_MAIN = 1792
SMALL_VR = 0
SMALL_IG = 32
SMALL_FG = 36

EXPERT_ROWS = 512
TOK_TILE = 256
LANES = 128
DMA_UNROLL = 8
VMEM_LIMIT = 48 * 1024 * 1024
FFN_VMEM_LIMIT = 56 * 1024 * 1024


def _cparams(sem):
    return pltpu.CompilerParams(dimension_semantics=sem, vmem_limit_bytes=VMEM_LIMIT)


def _mm(a, b):
    return jnp.dot(a.astype(BF16), b.astype(BF16), preferred_element_type=F32)


def _mm_nt(a, b):
    return lax.dot_general(a.astype(BF16), b.astype(BF16), (((1,), (1,)), ((), ())),
                           preferred_element_type=F32)


def _mm_tn(a, b):
    return jnp.dot(a.T.astype(BF16), b.astype(BF16), preferred_element_type=F32)


def _split3(x):
    hi = x.astype(BF16)
    r1 = x - hi.astype(F32)
    mid = r1.astype(BF16)
    lo = (r1 - mid.astype(F32)).astype(BF16)
    return hi, mid, lo


def _mm_exact_lhs(m, x):
    hi, mid, lo = _split3(x)
    return (jnp.dot(m, hi, preferred_element_type=F32) + jnp.dot(m, mid, preferred_element_type=F32)
            + jnp.dot(m, lo, preferred_element_type=F32))


def _mm_exact_rhs(x, m):
    hi, mid, lo = _split3(x)
    return (jnp.dot(hi, m, preferred_element_type=F32) + jnp.dot(mid, m, preferred_element_type=F32)
            + jnp.dot(lo, m, preferred_element_type=F32))


def _iota(shape, dim):
    return lax.broadcasted_iota(I32, shape, dim)


def _tri_incl(n):
    return jnp.where(_iota((n, n), 1) <= _iota((n, n), 0), 1.0, 0.0).astype(BF16)


def _seg_ones(n, seg):
    return jnp.where(_iota((n, n), 0) // seg == _iota((n, n), 1) // seg, 1.0, 0.0).astype(BF16)


def _sigmoid(x):
    return jax.nn.sigmoid(x)


def _silu(x):
    return x * jax.nn.sigmoid(x)


def _log_sigmoid(x):
    return jnp.minimum(x, 0.0) - jnp.log1p(jnp.exp(-jnp.abs(x)))


def _layernorm_rows(x, g, b):
    xc = x - jnp.mean(x, -1, keepdims=True)
    var = jnp.mean(xc * xc, -1, keepdims=True)
    return xc * lax.rsqrt(var + LN_EPS) * g + b


def _proj_kernel(x_ref, w_ref, o_ref):
    o_ref[...] = jnp.dot(x_ref[...], w_ref[...], preferred_element_type=F32)


def _proj(xb, w):
    n, k = xb.shape
    c = w.shape[1]
    tm = min(1024, n)
    tn = 1024
    return pl.pallas_call(
        _proj_kernel,
        grid=(c // tn, n // tm),
        in_specs=[pl.BlockSpec((tm, k), lambda j, i: (i, 0)),
                  pl.BlockSpec((k, tn), lambda j, i: (0, j))],
        out_specs=pl.BlockSpec((tm, tn), lambda j, i: (i, j)),
        out_shape=jax.ShapeDtypeStruct((n, c), F32),
        compiler_params=_cparams(("parallel", "parallel")),
        name="in_proj",
    )(xb, w)


def _rwkv_kernel(first, z_ref, vf_ref, mu_ref, w0_ref, w2_ref, a0_ref, a2_ref, v0_ref, v2_ref, g2_ref,
                 kk_ref, ka_ref, rk_ref, lng_ref, lnb_ref, y_ref, vo_ref, prev_ref, s_ref):
    t_len = CHUNK

    @pl.when(pl.program_id(1) == 0)
    def _():
        prev_ref[...] = jnp.zeros_like(prev_ref)
        s_ref[...] = jnp.zeros_like(s_ref)

    z = z_ref[:, 0:A_MAIN]
    small = z_ref[:, A_MAIN:A_MAIN + LANES]
    row = _iota((t_len, 1), 0)
    zs = jnp.where(row == 0, prev_ref[...], pltpu.roll(z, 1, 0))
    prev_ref[...] = z[t_len - 1:t_len, :]
    zz = z + (zs - z) * mu_ref[...]

    r = zz[:, 0:W]
    k = zz[:, W:2 * W]
    v = zz[:, 2 * W:3 * W]
    xwa = zz[:, 3 * W:3 * W + LANES]
    xg = zz[:, 3 * W + LANES:3 * W + 2 * LANES]

    u = w0_ref[...] + _mm(jnp.tanh(xwa), w2_ref[...])
    logw = -math.exp(-0.5) * _sigmoid(u)
    a_sig = _sigmoid(a0_ref[...] + _mm(xwa, a2_ref[...]))
    g = _mm(_sigmoid(xg), g2_ref[...])
    if not first:
        v = v + (vf_ref[...] - v) * _sigmoid(v0_ref[...] + _mm(small, v2_ref[...]))
    vo_ref[...] = v

    seg = _seg_ones(LANES, RWKV_HEAD_DIM)
    tri = _tri_incl(t_len)
    c = _mm_exact_lhs(tri, logw)
    c_last = c[t_len - 1:t_len, :]
    gam = jnp.exp(c)
    igam = jnp.exp(-c)
    gam_ex = jnp.exp(c - logw)
    tail = jnp.exp(c_last - c)
    g_last = jnp.exp(c_last)

    col64 = _iota((t_len, t_len), 1)
    row64 = _iota((t_len, t_len), 0)
    strict = col64 < row64
    incl = col64 <= row64
    lane = _iota((1, LANES), 1)
    bd_mask = _iota((LANES, LANES), 0) // RWKV_HEAD_DIM == _iota((LANES, LANES), 1) // RWKV_HEAD_DIM

    n_pair = W // LANES
    sls = [slice(p * LANES, (p + 1) * LANES) for p in range(n_pair)]
    hmasks = [lane < RWKV_HEAD_DIM, lane >= RWKV_HEAD_DIM]
    kk_raw = [k[:, sl] * kk_ref[:, sl] for sl in sls]
    ss = [_mm_exact_rhs(kr * kr, seg) for kr in kk_raw]
    kkn = [kr / jnp.maximum(jnp.sqrt(s_), 1e-12) for kr, s_ in zip(kk_raw, ss)]
    aps = [a_sig[:, sl] for sl in sls]
    k2s = [k[:, sl] * (1.0 + (ap - 1.0) * ka_ref[:, sl]) for sl, ap in zip(sls, aps)]
    rps = [r[:, sl] for sl in sls]
    vps = [v[:, sl] for sl in sls]
    r_ts = [rp * gam[:, sl] for rp, sl in zip(rps, sls)]
    a_ts = [-kn * gam_ex[:, sl] for kn, sl in zip(kkn, sls)]
    b_ts = [kn * ap * igam[:, sl] for kn, ap, sl in zip(kkn, aps, sls)]
    k_ts = [k2 * igam[:, sl] for k2, sl in zip(k2s, sls)]
    b_hats = [kn * ap * tail[:, sl] for kn, ap, sl in zip(kkn, aps, sls)]
    k_hats = [k2 * tail[:, sl] for k2, sl in zip(k2s, sls)]
    s_mats = [s_ref[p] for p in range(n_pair)]
    ah0 = [_mm_nt(a_ts[p], s_mats[p]) for p in range(n_pair)]
    rh0 = [_mm_nt(r_ts[p], s_mats[p]) for p in range(n_pair)]

    heads = [(p, hh) for p in range(n_pair) for hh in range(2)]
    lhs = [jnp.concatenate([jnp.where(hmasks[hh], a_ts[p], 0.0), jnp.where(hmasks[hh], r_ts[p], 0.0)], axis=0)
           for p, hh in heads]
    pb = [_mm_nt(lhs[i], b_ts[p]) for i, (p, hh) in enumerate(heads)]
    pk = [_mm_nt(lhs[i], k_ts[p]) for i, (p, hh) in enumerate(heads)]
    a_ak = [jnp.where(strict, m[0:t_len], 0.0) for m in pk]
    a_rb = [jnp.where(incl, m[t_len:], 0.0) for m in pb]
    a_rk = [jnp.where(incl, m[t_len:], 0.0) for m in pk]
    pw = [jnp.where(strict, m[0:t_len], 0.0) for m in pb]
    xs = [ah0[p] + _mm(a_ak[i], vps[p]) for i, (p, hh) in enumerate(heads)]
    for it in range(6):
        xs = [x + _mm(q_, x) for x, q_ in zip(xs, pw)]
        if it < 5:
            pw = [_mm(q_, q_) for q_ in pw]
    yh = [rh0[p] + _mm(a_rb[i], xs[i]) + _mm(a_rk[i], vps[p]) for i, (p, hh) in enumerate(heads)]
    u_pairs = [jnp.where(hmasks[0], xs[2 * p], xs[2 * p + 1]) for p in range(n_pair)]
    y_pairs = [jnp.where(hmasks[0], yh[2 * p], yh[2 * p + 1]) for p in range(n_pair)]

    upd = [_mm_tn(u_pairs[p], b_hats[p]) + _mm_tn(vps[p], k_hats[p]) for p in range(n_pair)]
    for p in range(n_pair):
        s_ref[p] = s_mats[p] * g_last[:, sls[p]] + jnp.where(bd_mask, upd[p], 0.0)

    means = [_mm_exact_rhs(y_, seg) * (1.0 / RWKV_HEAD_DIM) for y_ in y_pairs]
    ycs = [y_ - m_ for y_, m_ in zip(y_pairs, means)]
    vars_ = [_mm_exact_rhs(yc * yc, seg) * (1.0 / RWKV_HEAD_DIM) for yc in ycs]
    bonus = [_mm_exact_rhs(rps[p] * k2s[p] * rk_ref[:, sls[p]], seg) * vps[p] for p in range(n_pair)]
    for p in range(n_pair):
        sl = sls[p]
        yn = ycs[p] * lax.rsqrt(vars_[p] + RWKV_LNX_EPS) * lng_ref[:, sl] + lnb_ref[:, sl]
        y_ref[:, sl] = ((yn + bonus[p]) * g[:, sl]).astype(y_ref.dtype)


def _rwkv(z, v_first, prm, bsz, seq, first):
    n = bsz * seq
    nj = seq // CHUNK
    row_spec = lambda width: pl.BlockSpec((1, width), lambda b, j: (0, 0))
    mat_spec = lambda r_, c_: pl.BlockSpec((r_, c_), lambda b, j: (0, 0))
    tok = lambda width: pl.BlockSpec((CHUNK, width), lambda b, j: (b * nj + j, 0))
    in_specs = [
        pl.BlockSpec((CHUNK, 2048), lambda b, j: (b * nj + j, COL_A // 2048)),
        tok(W),
        row_spec(A_MAIN), row_spec(W), mat_spec(LANES, W), row_spec(W), mat_spec(LANES, W),
        row_spec(W), mat_spec(LANES, W), mat_spec(LANES, W),
        row_spec(W), row_spec(W), row_spec(W), row_spec(W), row_spec(W),
    ]
    return pl.pallas_call(
        functools.partial(_rwkv_kernel, first),
        grid=(bsz, nj),
        in_specs=in_specs,
        out_specs=[tok(W), tok(W)],
        out_shape=[jax.ShapeDtypeStruct((n, W), BF16), jax.ShapeDtypeStruct((n, W), F32)],
        scratch_shapes=[pltpu.VMEM((1, A_MAIN), F32), pltpu.VMEM((W // LANES, LANES, LANES), F32)],
        compiler_params=_cparams(("parallel", "arbitrary")),
        name="rwkv7",
    )(z, v_first, prm["mu"], prm["w0"], prm["w2"], prm["a0"], prm["a2"], prm["v0"], prm["v2"], prm["g2"],
      prm["k_k"], prm["k_a"], prm["r_k"], prm["lnx_g"], prm["lnx_b"])


HG_SUB = 16


def _hgrn_kernel(z_ref, lb_ref, nw_ref, y_ref, s_ref, q_s, k_s, v_s, bc_s, o_s):
    t_len = CHUNK

    @pl.when(pl.program_id(1) == 0)
    def _():
        s_ref[...] = jnp.zeros_like(s_ref)

    zq = z_ref[:, 0:W]
    f = z_ref[:, W:2 * W]
    lb = lb_ref[...]
    q_s[...] = _silu(zq)
    v_s[...] = z_ref[:, 2 * W:3 * W]
    la = jnp.log(lb)
    lbb = jnp.log1p(-lb) + _log_sigmoid(f)
    logf = jnp.maximum(la, lbb) + jnp.log1p(jnp.exp(-jnp.abs(la - lbb)))
    k_s[...] = (1.0 - lb) * _sigmoid(-f)
    blk = jnp.where((_iota((t_len, t_len), 1) <= _iota((t_len, t_len), 0))
                    & (_iota((t_len, t_len), 1) // HG_SUB == _iota((t_len, t_len), 0) // HG_SUB),
                    1.0, 0.0).astype(BF16)
    bc_s[...] = _mm_exact_lhs(blk, logf)

    row16 = _iota((HG_SUB, 1), 0)

    def sub_block(i):
        r0 = i * HG_SUB
        n_head = W // HEAD128
        sls = [slice(h * HEAD128, (h + 1) * HEAD128) for h in range(n_head)]
        qb = q_s[pl.ds(r0, HG_SUB), :]
        kb = k_s[pl.ds(r0, HG_SUB), :]
        vb = v_s[pl.ds(r0, HG_SUB), :]
        bcb = bc_s[pl.ds(r0, HG_SUB), :]
        st = [s_ref[h] for h in range(n_head)]
        qd = qb * jnp.exp(bcb)
        o = [_mm_nt(qd[:, sl], st[h]) for h, sl in enumerate(sls)]
        for s in range(HG_SUB):
            d = jnp.where(row16 >= s, bcb - bcb[s:s + 1, :], -jnp.inf)
            prod = qb * kb[s:s + 1, :] * jnp.exp(d)
            for h, sl in enumerate(sls):
                o[h] = o[h] + jnp.sum(prod[:, sl], axis=-1, keepdims=True) * vb[s:s + 1, sl]
        bl = bcb[HG_SUB - 1:HG_SUB, :]
        kd = kb * jnp.exp(bl - bcb)
        e_last = jnp.exp(bl)
        upd = [_mm_tn(vb[:, sl], kd[:, sl]) for sl in sls]
        for h, sl in enumerate(sls):
            s_ref[h] = st[h] * e_last[:, sl] + upd[h]
            o_s[pl.ds(r0, HG_SUB), sl] = o[h]

    for i in range(t_len // HG_SUB):
        sub_block(i)

    g = z_ref[:, 3 * W:4 * W]
    for h in range(W // HEAD128):
        sl = slice(h * HEAD128, (h + 1) * HEAD128)
        o = o_s[:, sl]
        o = o * lax.rsqrt(jnp.mean(o * o, -1, keepdims=True) + NORM_EPS) * nw_ref[:, sl]
        y_ref[:, sl] = (o * _silu(g[:, sl])).astype(y_ref.dtype)


def _hgrn(z, lb, norm_w, bsz, seq):
    n = bsz * seq
    nj = seq // CHUNK
    row_spec = pl.BlockSpec((1, W), lambda b, j: (0, 0))
    return pl.pallas_call(
        _hgrn_kernel,
        grid=(bsz, nj),
        in_specs=[pl.BlockSpec((CHUNK, 2048), lambda b, j: (b * nj + j, COL_B // 2048)), row_spec, row_spec],
        out_specs=pl.BlockSpec((CHUNK, W), lambda b, j: (b * nj + j, 0)),
        out_shape=jax.ShapeDtypeStruct((n, W), BF16),
        scratch_shapes=[pltpu.VMEM((W // HEAD128, HEAD128, HEAD128), F32)]
        + [pltpu.VMEM((CHUNK, W), F32) for _ in range(5)],
        compiler_params=_cparams(("parallel", "arbitrary")),
        name="hgrn2",
    )(z, lb, norm_w)


def _mlstm_kernel(z_ref, sm_ref, cw_ref, cb_ref, gb_ref, nw_ref, y_ref, ext_ref, c_ref, n_ref, m_ref):
    t_len = CHUNK
    pad = 8

    @pl.when(pl.program_id(1) == 0)
    def _():
        ext_ref[0:pad, :] = jnp.zeros((pad, 2 * W), F32)
        c_ref[...] = jnp.zeros_like(c_ref)
        n_ref[...] = jnp.zeros_like(n_ref)
        m_ref[...] = jnp.full(m_ref.shape, -1e30, F32)

    ext_ref[pad:pad + t_len, :] = z_ref[:, 0:2 * W]
    acc = cb_ref[...]
    for jj in range(CONV_WIDTH):
        acc = acc + ext_ref[pl.ds(pad - (CONV_WIDTH - 1) + jj, t_len), :] * cw_ref[jj:jj + 1, :]
    ext_ref[0:pad, :] = z_ref[t_len - pad:t_len, 0:2 * W]
    qk = _silu(acc)
    q = qk[:, 0:W]
    k = qk[:, W:2 * W] * (HEAD128 ** -0.5)
    v = z_ref[:, 2 * W:3 * W]
    og = z_ref[:, 3 * W:4 * W]

    gates = sm_ref[...] + gb_ref[...]
    lf = _log_sigmoid(gates)
    fc = _mm_exact_lhs(_tri_incl(t_len), lf)
    gates_t = gates.T
    fc_t = fc.T
    causal = _iota((t_len, t_len), 1) <= _iota((t_len, t_len), 0)

    hs = range(MLSTM_HEADS)
    sls = [slice(h * HEAD128, (h + 1) * HEAD128) for h in hs]
    ic_col = [gates[:, SMALL_IG + h:SMALL_IG + h + 1] for h in hs]
    ic_row = [gates_t[SMALL_IG + h:SMALL_IG + h + 1, :] for h in hs]
    fc_col = [fc[:, SMALL_FG + h:SMALL_FG + h + 1] for h in hs]
    fc_row = [fc_t[SMALL_FG + h:SMALL_FG + h + 1, :] for h in hs]
    m_prev = [m_ref[h:h + 1, 0:1] for h in hs]
    c_mat = [c_ref[h] for h in hs]
    n_vec = [n_ref[h:h + 1, :] for h in hs]
    qk_raw = [_mm_nt(q[:, sl], k[:, sl]) for sl in sls]
    q_c = [_mm(q[:, sl], c_mat[h]) for h, sl in enumerate(sls)]
    log_w = [jnp.where(causal, fc_col[h] - fc_row[h] + ic_row[h], -jnp.inf) for h in hs]
    log_inter = [fc_col[h] + m_prev[h] for h in hs]
    m_t = [jnp.maximum(log_inter[h], jnp.max(log_w[h], -1, keepdims=True)) for h in hs]
    w_inter = [jnp.exp(log_inter[h] - m_t[h]) for h in hs]
    s_qk = [qk_raw[h] * jnp.exp(log_w[h] - m_t[h]) for h in hs]
    s_v = [_mm(s_qk[h], v[:, sl]) for h, sl in enumerate(sls)]
    f_last = [fc_col[h][t_len - 1:t_len, :] for h in hs]
    log_s = [f_last[h] - fc_col[h] + ic_col[h] for h in hs]
    m_new = [jnp.maximum(f_last[h] + m_prev[h], jnp.max(log_s[h], 0, keepdims=True)) for h in hs]
    kw = [k[:, sl] * jnp.exp(log_s[h] - m_new[h]) for h, sl in enumerate(sls)]
    kv = [_mm_tn(kw[h], v[:, sl]) for h, sl in enumerate(sls)]
    for h, sl in enumerate(sls):
        num = w_inter[h] * q_c[h] + s_v[h]
        den = (w_inter[h] * jnp.sum(q[:, sl] * n_vec[h], -1, keepdims=True)
               + jnp.sum(s_qk[h], -1, keepdims=True))
        hid = num / jnp.maximum(jnp.abs(den), jnp.exp(-m_t[h]))
        decay = jnp.exp(f_last[h] + m_prev[h] - m_new[h])
        c_ref[h] = decay * c_mat[h] + kv[h]
        n_ref[h:h + 1, :] = decay * n_vec[h] + jnp.sum(kw[h], 0, keepdims=True)
        m_ref[h:h + 1, :] = jnp.broadcast_to(m_new[h], (1, LANES))
        hc = hid - jnp.mean(hid, -1, keepdims=True)
        hn = hc * lax.rsqrt(jnp.mean(hc * hc, -1, keepdims=True) + NORM_EPS) * nw_ref[:, sl]
        y_ref[:, sl] = (_sigmoid(og[:, sl]) * hn).astype(y_ref.dtype)


def _mlstm(z, conv_w, conv_b, gate_bias, norm_w, bsz, seq):
    n = bsz * seq
    nj = seq // CHUNK
    const = lambda r_, c_: pl.BlockSpec((r_, c_), lambda b, j: (0, 0))
    return pl.pallas_call(
        _mlstm_kernel,
        grid=(bsz, nj),
        in_specs=[pl.BlockSpec((CHUNK, 2048), lambda b, j: (b * nj + j, COL_C // 2048)),
                  pl.BlockSpec((CHUNK, LANES), lambda b, j: (b * nj + j, (COL_A + A_MAIN) // LANES)),
                  const(CONV_WIDTH, 2 * W), const(1, 2 * W), const(1, LANES), const(1, W)],
        out_specs=pl.BlockSpec((CHUNK, W), lambda b, j: (b * nj + j, 0)),
        out_shape=jax.ShapeDtypeStruct((n, W), BF16),
        scratch_shapes=[pltpu.VMEM((CHUNK + 8, 2 * W), F32),
                        pltpu.VMEM((MLSTM_HEADS, HEAD128, HEAD128), F32),
                        pltpu.VMEM((8, HEAD128), F32), pltpu.VMEM((8, LANES), F32)],
        compiler_params=_cparams(("parallel", "arbitrary")),
        name="mlstm",
    )(z, z, conv_w, conv_b, gate_bias, norm_w)


LRU_T = 256


def _lru_kernel(z_ref, cw_ref, cb_ref, wx_ref, bx_ref, wa_ref, ba_ref, lam_ref, y_ref, ext_ref, h_ref):
    t_len = LRU_T
    pad = 8
    first_blk = pl.program_id(1) == 0

    @pl.when(first_blk)
    def _():
        ext_ref[0:pad, :] = jnp.zeros((pad, W), F32)
        h_ref[...] = jnp.zeros_like(h_ref)

    ext_ref[pad:pad + t_len, :] = z_ref[:, 0:W]
    xc = cb_ref[...]
    for jj in range(CONV_WIDTH):
        xc = xc + ext_ref[pl.ds(pad - (CONV_WIDTH - 1) + jj, t_len), :] * cw_ref[jj:jj + 1, :]
    ext_ref[0:pad, :] = z_ref[t_len - pad:t_len, 0:W]

    gate_x = _sigmoid(_mm(xc, wx_ref[...]) + bx_ref[...])
    gate_a = _sigmoid(_mm(xc, wa_ref[...]) + ba_ref[...])
    log_a = LRU_C * gate_a * _log_sigmoid(lam_ref[...])
    a = jnp.exp(log_a)
    mult = jnp.sqrt(-jnp.tanh(log_a) * (a * a + 1.0))
    row = _iota((t_len, 1), 0)
    mult = jnp.where(jnp.logical_and(first_blk, row == 0), 1.0, mult)
    b = mult * gate_x * xc

    d = 1
    while d < t_len:
        keep = row >= d
        a_sh = jnp.where(keep, pltpu.roll(a, d, 0), 1.0)
        b_sh = jnp.where(keep, pltpu.roll(b, d, 0), 0.0)
        b = a * b_sh + b
        a = a * a_sh
        d *= 2
    h = a * h_ref[...] + b
    h_ref[...] = h[t_len - 1:t_len, :]
    y_ref[...] = (h * jax.nn.gelu(z_ref[:, W:2 * W], approximate=True)).astype(y_ref.dtype)


def _lru(z, conv_w, conv_b, wx, bx, wa, ba, lam, bsz, seq):
    n = bsz * seq
    nj = seq // LRU_T
    const = lambda r_, c_: pl.BlockSpec((r_, c_), lambda b, j: (0, 0))
    return pl.pallas_call(
        _lru_kernel,
        grid=(bsz, nj),
        in_specs=[pl.BlockSpec((LRU_T, 2 * W), lambda b, j: (b * nj + j, COL_D // (2 * W))),
                  const(CONV_WIDTH, W), const(1, W), const(W, W), const(1, W), const(W, W), const(1, W),
                  const(1, W)],
        out_specs=pl.BlockSpec((LRU_T, W), lambda b, j: (b * nj + j, 0)),
        out_shape=jax.ShapeDtypeStruct((n, W), BF16),
        scratch_shapes=[pltpu.VMEM((LRU_T + 8, W), F32), pltpu.VMEM((1, W), F32)],
        compiler_params=_cparams(("parallel", "arbitrary")),
        name="rglru",
    )(z, conv_w, conv_b, wx, bx, wa, ba, lam)


def _merge_kernel(x_ref, ya_ref, yb_ref, yc_ref, yd_ref, zg_ref, wb_ref, wo_ref, g1_ref, b1_ref,
                  p_ref, wpg_ref, wpp_ref, x1_ref, base_ref):
    merged = None
    for nb, y_ref in enumerate((ya_ref, yb_ref, yc_ref, yd_ref)):
        proj = jnp.dot(y_ref[...], wb_ref[nb], preferred_element_type=F32)
        term = _sigmoid(zg_ref[:, nb * D_MODEL:(nb + 1) * D_MODEL]) * proj
        merged = term if merged is None else merged + term
    mix = _mm(merged, wo_ref[...])
    x1 = _layernorm_rows(ALPHA * x_ref[...] + mix, g1_ref[...], b1_ref[...])
    x1_ref[...] = x1
    ple = _sigmoid(_mm(x1, wpg_ref[...])) * _mm(p_ref[...], wpp_ref[...])
    base_ref[...] = ALPHA * x1 + ple


def _merge(x, ys, z, w_branch, w_out, g1, b1, p_l, w_pg, w_pp):
    n = x.shape[0]
    tm = TOK_TILE
    tok = lambda width: pl.BlockSpec((tm, width), lambda i: (i, 0))
    const2 = lambda r_, c_: pl.BlockSpec((r_, c_), lambda i: (0, 0))
    return pl.pallas_call(
        _merge_kernel,
        grid=(n // tm,),
        in_specs=[tok(D_MODEL), tok(W), tok(W), tok(W), tok(W),
                  pl.BlockSpec((tm, 4 * D_MODEL), lambda i: (i, COL_G // (4 * D_MODEL))),
                  pl.BlockSpec((4, W, D_MODEL), lambda i: (0, 0, 0)), const2(D_MODEL, D_MODEL),
                  const2(1, D_MODEL), const2(1, D_MODEL), tok(D_PLE), const2(D_MODEL, D_MODEL),
                  const2(D_PLE, D_MODEL)],
        out_specs=[tok(D_MODEL), tok(D_MODEL)],
        out_shape=[jax.ShapeDtypeStruct((n, D_MODEL), F32), jax.ShapeDtypeStruct((n, D_MODEL), F32)],
        compiler_params=_cparams(("parallel",)),
        name="merge_ln1",
    )(x, *ys, z, w_branch, w_out, g1, b1, p_l, w_pg, w_pp)


ROUTE_IDX = 0
ROUTE_RANK = 4
ROUTE_GATE = 8


def _route_kernel(x_ref, rw_ref, rb_ref, out_ref, cnt_ref, carry_ref):
    tm = x_ref.shape[0]

    @pl.when(pl.program_id(0) == 0)
    def _():
        carry_ref[...] = jnp.zeros_like(carry_ref)

    xh, xm, _ = _split3(x_ref[...])
    wh, wm, _ = _split3(rw_ref[...])
    logits = (jnp.dot(xh, wh, preferred_element_type=F32) + jnp.dot(xh, wm, preferred_element_type=F32)
              + jnp.dot(xm, wh, preferred_element_type=F32)) + rb_ref[...]
    lane = _iota((tm, LANES), 1)
    lane_f = lane.astype(F32)
    cur = jnp.where(lane < N_EXPERTS, logits, -jnp.inf)
    vals, idxs, hots = [], [], []
    for _k in range(TOP_K):
        m = jnp.max(cur, -1, keepdims=True)
        ik = jnp.min(jnp.where(cur == m, lane_f, float(LANES)), -1, keepdims=True)
        hot = lane_f == ik
        vals.append(m)
        idxs.append(ik)
        hots.append(hot)
        cur = jnp.where(hot, -jnp.inf, cur)
    exps = [jnp.exp(vk - vals[0]) for vk in vals]
    denom = exps[0] + exps[1] + exps[2] + exps[3]
    count = jnp.zeros((tm, LANES), F32)
    for hot in hots:
        count = count + jnp.where(hot, 1.0, 0.0)
    strict = jnp.where(_iota((tm, tm), 1) < _iota((tm, tm), 0), 1.0, 0.0).astype(BF16)
    before = jnp.dot(strict, count.astype(BF16), preferred_element_type=F32) + carry_ref[...]
    out = jnp.zeros((tm, LANES), F32)
    for kk in range(TOP_K):
        rank = jnp.sum(jnp.where(hots[kk], before, 0.0), -1, keepdims=True)
        out = jnp.where(lane == ROUTE_IDX + kk, idxs[kk], out)
        out = jnp.where(lane == ROUTE_RANK + kk, rank, out)
        out = jnp.where(lane == ROUTE_GATE + kk, exps[kk] / denom, out)
    out_ref[...] = out
    total = carry_ref[...] + jnp.sum(count, 0, keepdims=True)
    carry_ref[...] = total
    cnt_ref[...] = total


def _route(x1, rw, rb):
    n = x1.shape[0]
    tm = TOK_TILE
    return pl.pallas_call(
        _route_kernel,
        grid=(n // tm,),
        in_specs=[pl.BlockSpec((tm, D_MODEL), lambda i: (i, 0)),
                  pl.BlockSpec((D_MODEL, LANES), lambda i: (0, 0)),
                  pl.BlockSpec((1, LANES), lambda i: (0, 0))],
        out_specs=[pl.BlockSpec((tm, LANES), lambda i: (i, 0)), pl.BlockSpec((1, LANES), lambda i: (0, 0))],
        out_shape=[jax.ShapeDtypeStruct((n, LANES), F32), jax.ShapeDtypeStruct((1, LANES), F32)],
        scratch_shapes=[pltpu.VMEM((1, LANES), F32)],
        compiler_params=_cparams(("arbitrary",)),
        name="router",
    )(x1, rw, rb)


def _row_copy(src, dst, sem):
    return pltpu.make_async_copy(src, dst, sem)


def _scatter_kernel(dest_ref, x_ref, xs_in_ref, xs_ref, sem):
    del xs_in_ref
    tm = x_ref.shape[0]

    def start(n, carry):
        for kk in range(TOP_K):
            d = dest_ref[n * TOP_K + kk]
            _row_copy(x_ref.at[pl.ds(n, 1), :], xs_ref.at[pl.ds(d, 1), :], sem).start(priority=kk % 2)
        return carry

    lax.fori_loop(0, tm, start, 0, unroll=DMA_UNROLL)

    def wait(n, carry):
        for kk in range(TOP_K):
            _row_copy(x_ref.at[pl.ds(0, 1), :], xs_ref.at[pl.ds(0, 1), :], sem).wait()
        return carry

    lax.fori_loop(0, tm, wait, 0, unroll=DMA_UNROLL)


def _scatter(dest, x1, n_slots):
    n = x1.shape[0]
    tm = TOK_TILE
    zeros = jnp.zeros((n_slots, D_MODEL), F32)
    return pl.pallas_call(
        _scatter_kernel,
        grid=(n // tm,),
        in_specs=[pl.BlockSpec((tm * TOP_K,), lambda i: (i,), memory_space=pltpu.SMEM),
                  pl.BlockSpec((tm, D_MODEL), lambda i: (i, 0)),
                  pl.BlockSpec(memory_space=pl.ANY)],
        out_specs=pl.BlockSpec(memory_space=pl.ANY),
        out_shape=jax.ShapeDtypeStruct((n_slots, D_MODEL), F32),
        scratch_shapes=[pltpu.SemaphoreType.DMA(())],
        input_output_aliases={2: 0},
        compiler_params=_cparams(("arbitrary",)),
        name="moe_scatter",
    )(dest, x1, zeros)


def _ffn_kernel(be_ref, nu_ref, xs_ref, wgu_ref, bgu_ref, wd_ref, bd_ref, ys_ref, wgu_b, wd_b):
    i = pl.program_id(0)
    live = i < nu_ref[0]
    fresh = jnp.logical_or(i == 0, be_ref[i] != be_ref[jnp.maximum(i - 1, 0)])

    @pl.when(jnp.logical_and(live, fresh))
    def _():
        wgu_b[...] = wgu_ref[0, 0].astype(BF16)
        wd_b[...] = wd_ref[0, 0].astype(BF16)

    @pl.when(live)
    def _():
        gu = jnp.dot(xs_ref[...].astype(BF16), wgu_b[...], preferred_element_type=F32) + bgu_ref[0, 0]
        gate = jnp.minimum(gu[:, 0:D_EXPERT], SWIGLU_LIMIT)
        up = jnp.clip(gu[:, D_EXPERT:], -SWIGLU_LIMIT, SWIGLU_LIMIT)
        glu = gate * _sigmoid(SWIGLU_ALPHA * gate)
        act = ((up + 1.0) * glu).astype(BF16)
        ys_ref[...] = jnp.dot(act, wd_b[...], preferred_element_type=F32) + bd_ref[0, 0]

    @pl.when(jnp.logical_not(live))
    def _():
        ys_ref[...] = jnp.zeros_like(ys_ref)


def _ffn(layer, block_e, n_used, xs, w_gu, b_gu, w_down, b_down):
    n_slots = xs.shape[0]
    bm = EXPERT_ROWS
    grid_spec = pltpu.PrefetchScalarGridSpec(
        num_scalar_prefetch=2,
        grid=(n_slots // bm,),
        in_specs=[pl.BlockSpec((bm, D_MODEL), lambda i, be, nu: (i, 0)),
                  pl.BlockSpec((1, 1, D_MODEL, 2 * D_EXPERT), lambda i, be, nu: (layer, be[i], 0, 0)),
                  pl.BlockSpec((1, 1, 1, 2 * D_EXPERT), lambda i, be, nu: (layer, be[i], 0, 0)),
                  pl.BlockSpec((1, 1, D_EXPERT, D_MODEL), lambda i, be, nu: (layer, be[i], 0, 0)),
                  pl.BlockSpec((1, 1, 1, D_MODEL), lambda i, be, nu: (layer, be[i], 0, 0))],
        out_specs=pl.BlockSpec((bm, D_MODEL), lambda i, be, nu: (i, 0)),
        scratch_shapes=[pltpu.VMEM((D_MODEL, 2 * D_EXPERT), BF16), pltpu.VMEM((D_EXPERT, D_MODEL), BF16)],
    )
    return pl.pallas_call(
        _ffn_kernel,
        grid_spec=grid_spec,
        out_shape=jax.ShapeDtypeStruct((n_slots, D_MODEL), F32),
        compiler_params=pltpu.CompilerParams(dimension_semantics=("arbitrary",),
                                             vmem_limit_bytes=FFN_VMEM_LIMIT),
        name="moe_ffn",
    )(block_e, n_used, xs, w_gu, b_gu, w_down, b_down)


def _combine_kernel(dest_ref, base_ref, route_ref, ys_ref, g2_ref, b2_ref, xo_ref, xb_ref, buf, sem):
    tm = base_ref.shape[0]

    def start(n, carry):
        for kk in range(TOP_K):
            d = dest_ref[n * TOP_K + kk]
            _row_copy(ys_ref.at[pl.ds(d, 1), :], buf.at[kk, pl.ds(n, 1), :], sem).start(priority=kk % 2)
        return carry

    lax.fori_loop(0, tm, start, 0, unroll=DMA_UNROLL)

    def wait(n, carry):
        for kk in range(TOP_K):
            _row_copy(ys_ref.at[pl.ds(0, 1), :], buf.at[0, pl.ds(0, 1), :], sem).wait()
        return carry

    lax.fori_loop(0, tm, wait, 0, unroll=DMA_UNROLL)

    acc = base_ref[...]
    for kk in range(TOP_K):
        acc = acc + route_ref[:, ROUTE_GATE + kk:ROUTE_GATE + kk + 1] * buf[kk]
    out = _layernorm_rows(acc, g2_ref[...], b2_ref[...])
    xo_ref[...] = out
    xb_ref[...] = out.astype(BF16)


def _combine(dest, base, route, ys, g2, b2):
    n = base.shape[0]
    tm = TOK_TILE
    tok = lambda width: pl.BlockSpec((tm, width), lambda i: (i, 0))
    return pl.pallas_call(
        _combine_kernel,
        grid=(n // tm,),
        in_specs=[pl.BlockSpec((tm * TOP_K,), lambda i: (i,), memory_space=pltpu.SMEM),
                  tok(D_MODEL), tok(LANES), pl.BlockSpec(memory_space=pl.ANY),
                  pl.BlockSpec((1, D_MODEL), lambda i: (0, 0)), pl.BlockSpec((1, D_MODEL), lambda i: (0, 0))],
        out_specs=[tok(D_MODEL), tok(D_MODEL)],
        out_shape=[jax.ShapeDtypeStruct((n, D_MODEL), F32), jax.ShapeDtypeStruct((n, D_MODEL), BF16)],
        scratch_shapes=[pltpu.VMEM((TOP_K, tm, D_MODEL), F32), pltpu.SemaphoreType.DMA(())],
        compiler_params=_cparams(("arbitrary",)),
        name="moe_combine_ln2",
    )(dest, base, route, ys, g2, b2)


def _row(v):
    return v.reshape(1, -1).astype(F32)


def _pad_rows(m, rows, offset):
    out = jnp.zeros((rows, m.shape[1]), m.dtype)
    return lax.dynamic_update_slice(out, m, (offset, 0))


def _regroup_w_in(w_in):
    d = w_in.shape[0]
    za = w_in[:, 0:A_MAIN]
    zb = w_in[:, 1792:3840]
    zc = w_in[:, 3840:5888]
    igfg = w_in[:, 5888:5896]
    zd = w_in[:, 5896:6920]
    zg = w_in[:, 6920:11016]
    vr = w_in[:, 11016:] if w_in.shape[1] > 11016 else jnp.zeros((d, 32), w_in.dtype)
    small = jnp.concatenate([vr, igfg, jnp.zeros((d, LANES - 40), w_in.dtype)], axis=1)
    a_grp = jnp.concatenate([za, small, jnp.zeros((d, 2048 - A_MAIN - LANES), w_in.dtype)], axis=1)
    return jnp.concatenate([zg, zb, zc, a_grp, zd], axis=1).astype(BF16)


def _block_diag(w):
    g, di, dj = w.shape
    eye = jnp.eye(g, dtype=w.dtype)
    return (eye[:, None, :, None] * w[:, :, None, :]).reshape(g * di, g * dj)


def kernel(x, p, w_in_first, w_in_rest, rwkv_mu, rwkv_w0, rwkv_w2, rwkv_a0, rwkv_a2, rwkv_v0, rwkv_v2,
           rwkv_g2, rwkv_k_k, rwkv_k_a, rwkv_r_k, rwkv_lnx_g, rwkv_lnx_b, hgrn_lower_bounds, hgrn_norm_w,
           mlstm_conv_w, mlstm_conv_b, mlstm_i_bias, mlstm_f_bias, mlstm_norm_w, lru_conv_w, lru_conv_b,
           lru_wx, lru_bx, lru_wa, lru_ba, lru_lambda, w_branch, w_out, ln1_g, ln1_b, router_w, router_b,
           expert_w_gu, expert_b_gu, expert_w_down, expert_b_down, ple_gate_w, ple_proj_w, ln2_g, ln2_b):
    bsz, seq, _ = x.shape
    n = bsz * seq
    n_assign = n * TOP_K
    n_blocks = n_assign // EXPERT_ROWS + N_EXPERTS
    n_slots = n_blocks * EXPERT_ROWS

    lb_all = jnp.cumsum(jax.nn.softmax(hgrn_lower_bounds.astype(F32), axis=0), axis=0)
    lb_all = lb_all - lb_all[0]

    xf = x.reshape(n, D_MODEL).astype(F32)
    xb = xf.astype(BF16)
    v_first = jnp.zeros((n, W), F32)
    for layer in range(DEPTH):
        first = layer == 0
        w_in = _regroup_w_in(w_in_first if first else w_in_rest[layer - 1])
        z = _proj(xb, w_in)

        rw_prm = {
            "mu": _row(rwkv_mu[layer]),
            "w0": _row(rwkv_w0[layer]),
            "w2": _pad_rows(rwkv_w2[layer], LANES, 0).astype(BF16),
            "a0": _row(rwkv_a0[layer]),
            "a2": _pad_rows(rwkv_a2[layer], LANES, 64).astype(BF16),
            "v0": _row(jnp.zeros((W,), F32) if first else rwkv_v0[layer - 1]),
            "v2": (jnp.zeros((LANES, W), BF16) if first
                   else _pad_rows(rwkv_v2[layer - 1], LANES, SMALL_VR).astype(BF16)),
            "g2": rwkv_g2[layer].astype(BF16),
            "k_k": _row(rwkv_k_k[layer]),
            "k_a": _row(rwkv_k_a[layer]),
            "r_k": _row(rwkv_r_k[layer]),
            "lnx_g": _row(rwkv_lnx_g[layer]),
            "lnx_b": _row(rwkv_lnx_b[layer]),
        }
        y_a, v_cur = _rwkv(z, v_first, rw_prm, bsz, seq, first)
        if first:
            v_first = v_cur
        y_b = _hgrn(z, _row(lb_all[layer]), _row(hgrn_norm_w[layer]), bsz, seq)
        gate_bias = jnp.zeros((LANES,), F32)
        gate_bias = gate_bias.at[SMALL_IG:SMALL_IG + MLSTM_HEADS].set(mlstm_i_bias[layer])
        gate_bias = gate_bias.at[SMALL_FG:SMALL_FG + MLSTM_HEADS].set(mlstm_f_bias[layer])
        y_c = _mlstm(z, mlstm_conv_w[layer].astype(F32), _row(mlstm_conv_b[layer]), _row(gate_bias),
                     _row(mlstm_norm_w[layer]), bsz, seq)
        y_d = _lru(z, lru_conv_w[layer].astype(F32), _row(lru_conv_b[layer]),
                   _block_diag(lru_wx[layer]).astype(BF16), _row(lru_bx[layer]),
                   _block_diag(lru_wa[layer]).astype(BF16), _row(lru_ba[layer]), _row(lru_lambda[layer]),
                   bsz, seq)

        x1, base = _merge(xf, (y_a, y_b, y_c, y_d), z, w_branch[layer].astype(BF16), w_out[layer].astype(BF16),
                          _row(ln1_g[layer]), _row(ln1_b[layer]), p[layer].reshape(n, D_PLE),
                          ple_gate_w[layer].astype(BF16), ple_proj_w[layer].astype(BF16))

        rw = jnp.zeros((D_MODEL, LANES), F32).at[:, 0:N_EXPERTS].set(router_w[layer])
        rb = jnp.zeros((1, LANES), F32).at[0, 0:N_EXPERTS].set(router_b[layer])
        route, counts_f = _route(x1, rw, rb)

        idx = route[:, ROUTE_IDX:ROUTE_IDX + TOP_K].astype(I32)
        rank = route[:, ROUTE_RANK:ROUTE_RANK + TOP_K].astype(I32)
        counts = counts_f[0, 0:N_EXPERTS].astype(I32)
        padded = (counts + EXPERT_ROWS - 1) // EXPERT_ROWS * EXPERT_ROWS
        pad_end = jnp.cumsum(padded)
        pad_start = pad_end - padded
        dest = (pad_start[idx] + rank).reshape(-1)
        block_start = jnp.arange(n_blocks, dtype=I32) * EXPERT_ROWS
        block_e = jnp.minimum(jnp.sum((pad_end[None, :] <= block_start[:, None]).astype(I32), axis=1),
                              N_EXPERTS - 1)
        n_used = (pad_end[-1:] // EXPERT_ROWS).astype(I32)

        xs = _scatter(dest, x1, n_slots)
        ys = _ffn(layer, block_e, n_used, xs, expert_w_gu.astype(F32),
                  expert_b_gu.reshape(DEPTH, N_EXPERTS, 1, 2 * D_EXPERT).astype(F32),
                  expert_w_down.astype(F32),
                  expert_b_down.reshape(DEPTH, N_EXPERTS, 1, D_MODEL).astype(F32))
        xf, xb = _combine(dest, base, route, ys, _row(ln2_g[layer]), _row(ln2_b[layer]))
    return xf.reshape(bsz, seq, D_MODEL).astype(x.dtype)
```

```python
import functools
import math

import jax
import jax.numpy as jnp
from jax import lax
from jax.experimental import pallas as pl
from jax.experimental.pallas import tpu as pltpu

F32 = jnp.float32
BF16 = jnp.bfloat16
I32 = jnp.int32

D_MODEL = 1024
DEPTH = 4
W = 512
RWKV_HEAD_DIM = 64
RWKV_LNX_EPS = 64e-5
MLSTM_HEADS = 4
HEAD128 = 128
CHUNK = 64
LRU_C = 8.0
CONV_WIDTH = 4
N_EXPERTS = 32
TOP_K = 4
D_EXPERT = 1024
SWIGLU_LIMIT = 7.0
SWIGLU_ALPHA = 1.702
D_PLE = 256
LN_EPS = 1e-5
NORM_EPS = 1e-6
ALPHA = (2 * DEPTH) ** 0.25

COL_G = 0
COL_B = 4096
COL_C = 6144
COL_A = 8192
COL_D = 10240
N_COLS = 11264
A_MAIN = 1792
SMALL_VR = 0
SMALL_IG = 32
SMALL_FG = 36

EXPERT_ROWS = 512
TOK_TILE = 256
LANES = 128
DMA_UNROLL = 8
VMEM_LIMIT = 48 * 1024 * 1024
FFN_VMEM_LIMIT = 56 * 1024 * 1024


def _cparams(sem):
    return pltpu.CompilerParams(dimension_semantics=sem, vmem_limit_bytes=VMEM_LIMIT)


def _mm(a, b):
    return jnp.dot(a.astype(BF16), b.astype(BF16), preferred_element_type=F32)


def _mm_nt(a, b):
    return lax.dot_general(a.astype(BF16), b.astype(BF16), (((1,), (1,)), ((), ())),
                           preferred_element_type=F32)


def _mm_tn(a, b):
    return jnp.dot(a.T.astype(BF16), b.astype(BF16), preferred_element_type=F32)


def _split3(x):
    hi = x.astype(BF16)
    r1 = x - hi.astype(F32)
    mid = r1.astype(BF16)
    lo = (r1 - mid.astype(F32)).astype(BF16)
    return hi, mid, lo


def _mm_exact_lhs(m, x):
    hi, mid, lo = _split3(x)
    return (jnp.dot(m, hi, preferred_element_type=F32) + jnp.dot(m, mid, preferred_element_type=F32)
            + jnp.dot(m, lo, preferred_element_type=F32))


def _mm_exact_rhs(x, m):
    hi, mid, lo = _split3(x)
    return (jnp.dot(hi, m, preferred_element_type=F32) + jnp.dot(mid, m, preferred_element_type=F32)
            + jnp.dot(lo, m, preferred_element_type=F32))


def _iota(shape, dim):
    return lax.broadcasted_iota(I32, shape, dim)


def _tri_incl(n):
    return jnp.where(_iota((n, n), 1) <= _iota((n, n), 0), 1.0, 0.0).astype(BF16)


def _seg_ones(n, seg):
    return jnp.where(_iota((n, n), 0) // seg == _iota((n, n), 1) // seg, 1.0, 0.0).astype(BF16)


def _sigmoid(x):
    return jax.nn.sigmoid(x)


def _silu(x):
    return x * jax.nn.sigmoid(x)


def _log_sigmoid(x):
    return jnp.minimum(x, 0.0) - jnp.log1p(jnp.exp(-jnp.abs(x)))


def _layernorm_rows(x, g, b):
    xc = x - jnp.mean(x, -1, keepdims=True)
    var = jnp.mean(xc * xc, -1, keepdims=True)
    return xc * lax.rsqrt(var + LN_EPS) * g + b


def _proj_kernel(x_ref, w_ref, o_ref):
    o_ref[...] = jnp.dot(x_ref[...], w_ref[...], preferred_element_type=F32).astype(o_ref.dtype)


def _proj(xb, w):
    n, k = xb.shape
    c = w.shape[1]
    tm = min(1024, n)
    tn = 1024
    return pl.pallas_call(
        _proj_kernel,
        grid=(c // tn, n // tm),
        in_specs=[pl.BlockSpec((tm, k), lambda j, i: (i, 0)),
                  pl.BlockSpec((k, tn), lambda j, i: (0, j))],
        out_specs=pl.BlockSpec((tm, tn), lambda j, i: (i, j)),
        out_shape=jax.ShapeDtypeStruct((n, c), BF16),
        compiler_params=_cparams(("parallel", "parallel")),
        name="in_proj",
    )(xb, w)


RWKV_CHUNKS = 4
RWKV_T = CHUNK * RWKV_CHUNKS


def _rwkv_kernel(first, z_ref, vf_ref, mu_ref, w0_ref, w2_ref, a0_ref, a2_ref, v0_ref, v2_ref, g2_ref,
                 kk_ref, ka_ref, rk_ref, lng_ref, lnb_ref, y_ref, vo_ref, prev_ref, s_ref):
    t_len = RWKV_T

    @pl.when(pl.program_id(1) == 0)
    def _():
        prev_ref[...] = jnp.zeros_like(prev_ref)
        s_ref[...] = jnp.zeros_like(s_ref)

    z = z_ref[:, 0:A_MAIN].astype(F32)
    small = z_ref[:, A_MAIN:A_MAIN + LANES].astype(F32)
    row = _iota((t_len, 1), 0)
    zs = jnp.where(row == 0, prev_ref[...], pltpu.roll(z, 1, 0))
    prev_ref[...] = z[t_len - 1:t_len, :]
    zz = z + (zs - z) * mu_ref[...]

    r = zz[:, 0:W]
    k = zz[:, W:2 * W]
    v = zz[:, 2 * W:3 * W]
    xwa = zz[:, 3 * W:3 * W + LANES]
    xg = zz[:, 3 * W + LANES:3 * W + 2 * LANES]

    u = w0_ref[...] + _mm(jnp.tanh(xwa), w2_ref[...])
    logw = -math.exp(-0.5) * _sigmoid(u)
    a_sig = _sigmoid(a0_ref[...] + _mm(xwa, a2_ref[...]))
    g = _mm(_sigmoid(xg), g2_ref[...])
    if not first:
        v = v + (vf_ref[...] - v) * _sigmoid(v0_ref[...] + _mm(small, v2_ref[...]))
    vo_ref[...] = v

    seg = _seg_ones(LANES, RWKV_HEAD_DIM)
    tri = jnp.where((_iota((t_len, t_len), 1) <= _iota((t_len, t_len), 0))
                    & (_iota((t_len, t_len), 1) // CHUNK == _iota((t_len, t_len), 0) // CHUNK),
                    1.0, 0.0).astype(BF16)
    c = _mm_exact_lhs(tri, logw)
    gam = jnp.exp(c)
    igam = jnp.exp(-c)
    gam_ex = jnp.exp(c - logw)

    lane = _iota((1, LANES), 1)
    lane2 = _iota((1, 2 * LANES), 1)
    hm = [lane < RWKV_HEAD_DIM, lane >= RWKV_HEAD_DIM]
    hm2 = [lane2 % LANES < RWKV_HEAD_DIM, lane2 % LANES >= RWKV_HEAD_DIM]
    bd_mask = _iota((LANES, LANES), 0) // RWKV_HEAD_DIM == _iota((LANES, LANES), 1) // RWKV_HEAD_DIM
    pstrict = (_iota((CHUNK, LANES), 1) % RWKV_HEAD_DIM) < _iota((CHUNK, LANES), 0)
    pincl = (_iota((CHUNK, LANES), 1) % RWKV_HEAD_DIM) <= _iota((CHUNK, LANES), 0)

    def bdiag(m, masks=hm):
        return jnp.concatenate([jnp.where(masks[0], m, 0.0), jnp.where(masks[1], m, 0.0)], axis=0)

    n_pair = W // LANES
    sls = [slice(p * LANES, (p + 1) * LANES) for p in range(n_pair)]
    rss = [slice(ci * CHUNK, (ci + 1) * CHUNK) for ci in range(RWKV_CHUNKS)]
    kk_raw = [k[:, sl] * kk_ref[:, sl] for sl in sls]
    kkn = [kr / jnp.maximum(jnp.sqrt(_mm_exact_rhs(kr * kr, seg)), 1e-12) for kr in kk_raw]
    aps = [a_sig[:, sl] for sl in sls]
    k2s = [k[:, sl] * (1.0 + (ap - 1.0) * ka_ref[:, sl]) for sl, ap in zip(sls, aps)]
    r_ts = [r[:, sl] * gam[:, sl] for sl in sls]
    a_ts = [-kn * gam_ex[:, sl] for kn, sl in zip(kkn, sls)]
    kb = [kn * ap for kn, ap in zip(kkn, aps)]
    b_ts = [x_ * igam[:, sl] for x_, sl in zip(kb, sls)]
    k_ts = [k2 * igam[:, sl] for k2, sl in zip(k2s, sls)]

    cps = [(ci, p) for ci in range(RWKV_CHUNKS) for p in range(n_pair)]
    lhs = [jnp.concatenate([a_ts[p][rss[ci]], r_ts[p][rss[ci]]], axis=0) for ci, p in cps]
    pb = [_mm_nt(lhs[i], bdiag(b_ts[p][rss[ci]])) for i, (ci, p) in enumerate(cps)]
    pk = [_mm_nt(lhs[i], bdiag(k_ts[p][rss[ci]])) for i, (ci, p) in enumerate(cps)]
    v_bd = [bdiag(v[rss[ci], sls[p]]) for ci, p in cps]
    a_ak = [jnp.where(pstrict, m[0:CHUNK], 0.0) for m in pk]
    a_rb = [jnp.where(pincl, m[CHUNK:], 0.0) for m in pb]
    a_rk = [jnp.where(pincl, m[CHUNK:], 0.0) for m in pk]
    pw = [jnp.where(pstrict, m[0:CHUNK], 0.0) for m in pb]
    w2 = [jnp.concatenate([_mm(a_ak[i], v_bd[i]), a_ts[p][rss[ci]]], axis=1) for i, (ci, p) in enumerate(cps)]
    for it in range(6):
        w2 = [x_ + _mm(q_, bdiag(x_, hm2)) for x_, q_ in zip(w2, pw)]
        if it < 5:
            pw = [_mm(q_, bdiag(q_)) for q_ in pw]

    s_cur = [s_ref[p] for p in range(n_pair)]
    y_rows = [[] for _ in range(n_pair)]
    for ci in range(RWKV_CHUNKS):
        rs = rss[ci]
        idx = [ci * n_pair + p for p in range(n_pair)]
        c_last = c[ci * CHUNK + CHUNK - 1:ci * CHUNK + CHUNK, :]
        tail = jnp.exp(c_last - c[rs])
        g_last = jnp.exp(c_last)
        ls0 = [_mm_nt(jnp.concatenate([w2[idx[p]][:, LANES:], r_ts[p][rs]], axis=0), s_cur[p])
               for p in range(n_pair)]
        xs = [w2[idx[p]][:, 0:LANES] + ls0[p][0:CHUNK] for p in range(n_pair)]
        ys = [ls0[p][CHUNK:] + _mm(jnp.concatenate([a_rb[idx[p]], a_rk[idx[p]]], axis=1),
                                   jnp.concatenate([bdiag(xs[p]), v_bd[idx[p]]], axis=0))
              for p in range(n_pair)]
        upd = [_mm_tn(jnp.concatenate([xs[p], v[rs, sls[p]]], axis=0),
                      jnp.concatenate([kb[p][rs] * tail[:, sls[p]], k2s[p][rs] * tail[:, sls[p]]], axis=0))
               for p in range(n_pair)]
        s_cur = [s_cur[p] * g_last[:, sls[p]] + jnp.where(bd_mask, upd[p], 0.0) for p in range(n_pair)]
        for p in range(n_pair):
            y_rows[p].append(ys[p])
    for p in range(n_pair):
        s_ref[p] = s_cur[p]

    y_pairs = [jnp.concatenate(rows_, axis=0) for rows_ in y_rows]
    means = [_mm_exact_rhs(y_, seg) * (1.0 / RWKV_HEAD_DIM) for y_ in y_pairs]
    ycs = [y_ - m_ for y_, m_ in zip(y_pairs, means)]
    vars_ = [_mm_exact_rhs(yc * yc, seg) * (1.0 / RWKV_HEAD_DIM) for yc in ycs]
    bonus = [_mm_exact_rhs(r[:, sls[p]] * k2s[p] * rk_ref[:, sls[p]], seg) * v[:, sls[p]] for p in range(n_pair)]
    for p in range(n_pair):
        sl = sls[p]
        yn = ycs[p] * lax.rsqrt(vars_[p] + RWKV_LNX_EPS) * lng_ref[:, sl] + lnb_ref[:, sl]
        y_ref[:, sl] = ((yn + bonus[p]) * g[:, sl]).astype(y_ref.dtype)


def _rwkv(z, v_first, prm, bsz, seq, first):
    n = bsz * seq
    nj = seq // RWKV_T
    row_spec = lambda width: pl.BlockSpec((1, width), lambda b, j: (0, 0))
    mat_spec = lambda r_, c_: pl.BlockSpec((r_, c_), lambda b, j: (0, 0))
    tok = lambda width: pl.BlockSpec((RWKV_T, width), lambda b, j: (b * nj + j, 0))
    in_specs = [
        pl.BlockSpec((RWKV_T, 2048), lambda b, j: (b * nj + j, COL_A // 2048)),
        tok(W),
        row_spec(A_MAIN), row_spec(W), mat_spec(LANES, W), row_spec(W), mat_spec(LANES, W),
        row_spec(W), mat_spec(LANES, W), mat_spec(LANES, W),
        row_spec(W), row_spec(W), row_spec(W), row_spec(W), row_spec(W),
    ]
    return pl.pallas_call(
        functools.partial(_rwkv_kernel, first),
        grid=(bsz, nj),
        in_specs=in_specs,
        out_specs=[tok(W), tok(W)],
        out_shape=[jax.ShapeDtypeStruct((n, W), BF16), jax.ShapeDtypeStruct((n, W), F32)],
        scratch_shapes=[pltpu.VMEM((1, A_MAIN), F32), pltpu.VMEM((W // LANES, LANES, LANES), F32)],
        compiler_params=_cparams(("parallel", "arbitrary")),
        name="rwkv7",
    )(z, v_first, prm["mu"], prm["w0"], prm["w2"], prm["a0"], prm["a2"], prm["v0"], prm["v2"], prm["g2"],
      prm["k_k"], prm["k_a"], prm["r_k"], prm["lnx_g"], prm["lnx_b"])


HG_SUB = 16


def _hgrn_kernel(z_ref, lb_ref, nw_ref, y_ref, s_ref, q_s, k_s, v_s, bc_s, o_s):
    t_len = CHUNK

    @pl.when(pl.program_id(1) == 0)
    def _():
        s_ref[...] = jnp.zeros_like(s_ref)

    zq = z_ref[:, 0:W].astype(F32)
    f = z_ref[:, W:2 * W].astype(F32)
    lb = lb_ref[...]
    q_s[...] = _silu(zq)
    v_s[...] = z_ref[:, 2 * W:3 * W].astype(F32)
    la = jnp.log(lb)
    lbb = jnp.log1p(-lb) + _log_sigmoid(f)
    logf = jnp.maximum(la, lbb) + jnp.log1p(jnp.exp(-jnp.abs(la - lbb)))
    k_s[...] = (1.0 - lb) * _sigmoid(-f)
    blk = jnp.where((_iota((t_len, t_len), 1) <= _iota((t_len, t_len), 0))
                    & (_iota((t_len, t_len), 1) // HG_SUB == _iota((t_len, t_len), 0) // HG_SUB),
                    1.0, 0.0).astype(BF16)
    bc_s[...] = _mm_exact_lhs(blk, logf)

    row16 = _iota((HG_SUB, 1), 0)

    def sub_block(i):
        r0 = i * HG_SUB
        n_head = W // HEAD128
        sls = [slice(h * HEAD128, (h + 1) * HEAD128) for h in range(n_head)]
        qb = q_s[pl.ds(r0, HG_SUB), :]
        kb = k_s[pl.ds(r0, HG_SUB), :]
        vb = v_s[pl.ds(r0, HG_SUB), :]
        bcb = bc_s[pl.ds(r0, HG_SUB), :]
        st = [s_ref[h] for h in range(n_head)]
        qd = qb * jnp.exp(bcb)
        o = [_mm_nt(qd[:, sl], st[h]) for h, sl in enumerate(sls)]
        for s in range(HG_SUB):
            d = jnp.where(row16 >= s, bcb - bcb[s:s + 1, :], -jnp.inf)
            prod = qb * kb[s:s + 1, :] * jnp.exp(d)
            for h, sl in enumerate(sls):
                o[h] = o[h] + jnp.sum(prod[:, sl], axis=-1, keepdims=True) * vb[s:s + 1, sl]
        bl = bcb[HG_SUB - 1:HG_SUB, :]
        kd = kb * jnp.exp(bl - bcb)
        e_last = jnp.exp(bl)
        upd = [_mm_tn(vb[:, sl], kd[:, sl]) for sl in sls]
        for h, sl in enumerate(sls):
            s_ref[h] = st[h] * e_last[:, sl] + upd[h]
            o_s[pl.ds(r0, HG_SUB), sl] = o[h]

    for i in range(t_len // HG_SUB):
        sub_block(i)

    g = z_ref[:, 3 * W:4 * W].astype(F32)
    for h in range(W // HEAD128):
        sl = slice(h * HEAD128, (h + 1) * HEAD128)
        o = o_s[:, sl]
        o = o * lax.rsqrt(jnp.mean(o * o, -1, keepdims=True) + NORM_EPS) * nw_ref[:, sl]
        y_ref[:, sl] = (o * _silu(g[:, sl])).astype(y_ref.dtype)


def _hgrn(z, lb, norm_w, bsz, seq):
    n = bsz * seq
    nj = seq // CHUNK
    row_spec = pl.BlockSpec((1, W), lambda b, j: (0, 0))
    return pl.pallas_call(
        _hgrn_kernel,
        grid=(bsz, nj),
        in_specs=[pl.BlockSpec((CHUNK, 2048), lambda b, j: (b * nj + j, COL_B // 2048)), row_spec, row_spec],
        out_specs=pl.BlockSpec((CHUNK, W), lambda b, j: (b * nj + j, 0)),
        out_shape=jax.ShapeDtypeStruct((n, W), BF16),
        scratch_shapes=[pltpu.VMEM((W // HEAD128, HEAD128, HEAD128), F32)]
        + [pltpu.VMEM((CHUNK, W), F32) for _ in range(5)],
        compiler_params=_cparams(("parallel", "arbitrary")),
        name="hgrn2",
    )(z, lb, norm_w)


def _mlstm_kernel(z_ref, sm_ref, cw_ref, cb_ref, gb_ref, nw_ref, y_ref, ext_ref, c_ref, n_ref, m_ref):
    t_len = CHUNK
    pad = 8

    @pl.when(pl.program_id(1) == 0)
    def _():
        ext_ref[0:pad, :] = jnp.zeros((pad, 2 * W), F32)
        c_ref[...] = jnp.zeros_like(c_ref)
        n_ref[...] = jnp.zeros_like(n_ref)
        m_ref[...] = jnp.full(m_ref.shape, -1e30, F32)

    ext_ref[pad:pad + t_len, :] = z_ref[:, 0:2 * W].astype(F32)
    acc = cb_ref[...]
    for jj in range(CONV_WIDTH):
        acc = acc + ext_ref[pl.ds(pad - (CONV_WIDTH - 1) + jj, t_len), :] * cw_ref[jj:jj + 1, :]
    ext_ref[0:pad, :] = ext_ref[t_len:t_len + pad, :]
    qk = _silu(acc)
    q = qk[:, 0:W]
    k = qk[:, W:2 * W] * (HEAD128 ** -0.5)
    v = z_ref[:, 2 * W:3 * W].astype(F32)
    og = z_ref[:, 3 * W:4 * W].astype(F32)

    gates = sm_ref[...].astype(F32) + gb_ref[...]
    lf = _log_sigmoid(gates)
    fc = _mm_exact_lhs(_tri_incl(t_len), lf)
    gates_t = gates.T
    fc_t = fc.T
    causal = _iota((t_len, t_len), 1) <= _iota((t_len, t_len), 0)

    hs = range(MLSTM_HEADS)
    sls = [slice(h * HEAD128, (h + 1) * HEAD128) for h in hs]
    ic_col = [gates[:, SMALL_IG + h:SMALL_IG + h + 1] for h in hs]
    ic_row = [gates_t[SMALL_IG + h:SMALL_IG + h + 1, :] for h in hs]
    fc_col = [fc[:, SMALL_FG + h:SMALL_FG + h + 1] for h in hs]
    fc_row = [fc_t[SMALL_FG + h:SMALL_FG + h + 1, :] for h in hs]
    m_prev = [m_ref[h:h + 1, 0:1] for h in hs]
    c_mat = [c_ref[h] for h in hs]
    n_vec = [n_ref[h:h + 1, :] for h in hs]
    qk_raw = [_mm_nt(q[:, sl], k[:, sl]) for sl in sls]
    q_c = [_mm(q[:, sl], c_mat[h]) for h, sl in enumerate(sls)]
    log_w = [jnp.where(causal, fc_col[h] - fc_row[h] + ic_row[h], -jnp.inf) for h in hs]
    log_inter = [fc_col[h] + m_prev[h] for h in hs]
    m_t = [jnp.maximum(log_inter[h], jnp.max(log_w[h], -1, keepdims=True)) for h in hs]
    w_inter = [jnp.exp(log_inter[h] - m_t[h]) for h in hs]
    s_qk = [qk_raw[h] * jnp.exp(log_w[h] - m_t[h]) for h in hs]
    s_v = [_mm(s_qk[h], v[:, sl]) for h, sl in enumerate(sls)]
    f_last = [fc_col[h][t_len - 1:t_len, :] for h in hs]
    log_s = [f_last[h] - fc_col[h] + ic_col[h] for h in hs]
    m_new = [jnp.maximum(f_last[h] + m_prev[h], jnp.max(log_s[h], 0, keepdims=True)) for h in hs]
    kw = [k[:, sl] * jnp.exp(log_s[h] - m_new[h]) for h, sl in enumerate(sls)]
    kv = [_mm_tn(kw[h], v[:, sl]) for h, sl in enumerate(sls)]
    for h, sl in enumerate(sls):
        num = w_inter[h] * q_c[h] + s_v[h]
        den = (w_inter[h] * jnp.sum(q[:, sl] * n_vec[h], -1, keepdims=True)
               + jnp.sum(s_qk[h], -1, keepdims=True))
        hid = num / jnp.maximum(jnp.abs(den), jnp.exp(-m_t[h]))
        decay = jnp.exp(f_last[h] + m_prev[h] - m_new[h])
        c_ref[h] = decay * c_mat[h] + kv[h]
        n_ref[h:h + 1, :] = decay * n_vec[h] + jnp.sum(kw[h], 0, keepdims=True)
        m_ref[h:h + 1, :] = jnp.broadcast_to(m_new[h], (1, LANES))
        hc = hid - jnp.mean(hid, -1, keepdims=True)
        hn = hc * lax.rsqrt(jnp.mean(hc * hc, -1, keepdims=True) + NORM_EPS) * nw_ref[:, sl]
        y_ref[:, sl] = (_sigmoid(og[:, sl]) * hn).astype(y_ref.dtype)


def _mlstm(z, conv_w, conv_b, gate_bias, norm_w, bsz, seq):
    n = bsz * seq
    nj = seq // CHUNK
    const = lambda r_, c_: pl.BlockSpec((r_, c_), lambda b, j: (0, 0))
    return pl.pallas_call(
        _mlstm_kernel,
        grid=(bsz, nj),
        in_specs=[pl.BlockSpec((CHUNK, 2048), lambda b, j: (b * nj + j, COL_C // 2048)),
                  pl.BlockSpec((CHUNK, LANES), lambda b, j: (b * nj + j, (COL_A + A_MAIN) // LANES)),
                  const(CONV_WIDTH, 2 * W), const(1, 2 * W), const(1, LANES), const(1, W)],
        out_specs=pl.BlockSpec((CHUNK, W), lambda b, j: (b * nj + j, 0)),
        out_shape=jax.ShapeDtypeStruct((n, W), BF16),
        scratch_shapes=[pltpu.VMEM((CHUNK + 8, 2 * W), F32),
                        pltpu.VMEM((MLSTM_HEADS, HEAD128, HEAD128), F32),
                        pltpu.VMEM((8, HEAD128), F32), pltpu.VMEM((8, LANES), F32)],
        compiler_params=_cparams(("parallel", "arbitrary")),
        name="mlstm",
    )(z, z, conv_w, conv_b, gate_bias, norm_w)


LRU_T = 256


def _lru_kernel(z_ref, cw_ref, cb_ref, wx_ref, bx_ref, wa_ref, ba_ref, lam_ref, y_ref, ext_ref, h_ref):
    t_len = LRU_T
    pad = 8
    first_blk = pl.program_id(1) == 0

    @pl.when(first_blk)
    def _():
        ext_ref[0:pad, :] = jnp.zeros((pad, W), F32)
        h_ref[...] = jnp.zeros_like(h_ref)

    ext_ref[pad:pad + t_len, :] = z_ref[:, 0:W].astype(F32)
    xc = cb_ref[...]
    for jj in range(CONV_WIDTH):
        xc = xc + ext_ref[pl.ds(pad - (CONV_WIDTH - 1) + jj, t_len), :] * cw_ref[jj:jj + 1, :]
    ext_ref[0:pad, :] = ext_ref[t_len:t_len + pad, :]

    gate_x = _sigmoid(_mm(xc, wx_ref[...]) + bx_ref[...])
    gate_a = _sigmoid(_mm(xc, wa_ref[...]) + ba_ref[...])
    log_a = LRU_C * gate_a * _log_sigmoid(lam_ref[...])
    a = jnp.exp(log_a)
    mult = jnp.sqrt(-jnp.tanh(log_a) * (a * a + 1.0))
    row = _iota((t_len, 1), 0)
    mult = jnp.where(jnp.logical_and(first_blk, row == 0), 1.0, mult)
    b = mult * gate_x * xc

    d = 1
    while d < t_len:
        keep = row >= d
        a_sh = jnp.where(keep, pltpu.roll(a, d, 0), 1.0)
        b_sh = jnp.where(keep, pltpu.roll(b, d, 0), 0.0)
        b = a * b_sh + b
        a = a * a_sh
        d *= 2
    h = a * h_ref[...] + b
    h_ref[...] = h[t_len - 1:t_len, :]
    y_ref[...] = (h * jax.nn.gelu(z_ref[:, W:2 * W].astype(F32), approximate=True)).astype(y_ref.dtype)


def _lru(z, conv_w, conv_b, wx, bx, wa, ba, lam, bsz, seq):
    n = bsz * seq
    nj = seq // LRU_T
    const = lambda r_, c_: pl.BlockSpec((r_, c_), lambda b, j: (0, 0))
    return pl.pallas_call(
        _lru_kernel,
        grid=(bsz, nj),
        in_specs=[pl.BlockSpec((LRU_T, 2 * W), lambda b, j: (b * nj + j, COL_D // (2 * W))),
                  const(CONV_WIDTH, W), const(1, W), const(W, W), const(1, W), const(W, W), const(1, W),
                  const(1, W)],
        out_specs=pl.BlockSpec((LRU_T, W), lambda b, j: (b * nj + j, 0)),
        out_shape=jax.ShapeDtypeStruct((n, W), BF16),
        scratch_shapes=[pltpu.VMEM((LRU_T + 8, W), F32), pltpu.VMEM((1, W), F32)],
        compiler_params=_cparams(("parallel", "arbitrary")),
        name="rglru",
    )(z, conv_w, conv_b, wx, bx, wa, ba, lam)


def _merge_kernel(x_ref, ya_ref, yb_ref, yc_ref, yd_ref, zg_ref, wb_ref, wo_ref, g1_ref, b1_ref,
                  p_ref, wpg_ref, wpp_ref, x1_ref, base_ref):
    merged = None
    for nb, y_ref in enumerate((ya_ref, yb_ref, yc_ref, yd_ref)):
        proj = jnp.dot(y_ref[...], wb_ref[nb], preferred_element_type=F32)
        term = _sigmoid(zg_ref[:, nb * D_MODEL:(nb + 1) * D_MODEL].astype(F32)) * proj
        merged = term if merged is None else merged + term
    mix = _mm(merged, wo_ref[...])
    x1 = _layernorm_rows(ALPHA * x_ref[...] + mix, g1_ref[...], b1_ref[...])
    x1_ref[...] = x1
    ple = _sigmoid(_mm(x1, wpg_ref[...])) * _mm(p_ref[...], wpp_ref[...])
    base_ref[...] = ALPHA * x1 + ple


def _merge(x, ys, z, w_branch, w_out, g1, b1, p_l, w_pg, w_pp):
    n = x.shape[0]
    tm = TOK_TILE
    tok = lambda width: pl.BlockSpec((tm, width), lambda i: (i, 0))
    const2 = lambda r_, c_: pl.BlockSpec((r_, c_), lambda i: (0, 0))
    return pl.pallas_call(
        _merge_kernel,
        grid=(n // tm,),
        in_specs=[tok(D_MODEL), tok(W), tok(W), tok(W), tok(W),
                  pl.BlockSpec((tm, 4 * D_MODEL), lambda i: (i, COL_G // (4 * D_MODEL))),
                  pl.BlockSpec((4, W, D_MODEL), lambda i: (0, 0, 0)), const2(D_MODEL, D_MODEL),
                  const2(1, D_MODEL), const2(1, D_MODEL), tok(D_PLE), const2(D_MODEL, D_MODEL),
                  const2(D_PLE, D_MODEL)],
        out_specs=[tok(D_MODEL), tok(D_MODEL)],
        out_shape=[jax.ShapeDtypeStruct((n, D_MODEL), F32), jax.ShapeDtypeStruct((n, D_MODEL), F32)],
        compiler_params=_cparams(("parallel",)),
        name="merge_ln1",
    )(x, *ys, z, w_branch, w_out, g1, b1, p_l, w_pg, w_pp)


ROUTE_IDX = 0
ROUTE_RANK = 4
ROUTE_GATE = 8


def _route_kernel(x_ref, rw_ref, rb_ref, out_ref, cnt_ref, carry_ref):
    tm = x_ref.shape[0]

    @pl.when(pl.program_id(0) == 0)
    def _():
        carry_ref[...] = jnp.zeros_like(carry_ref)

    xh, xm, _ = _split3(x_ref[...])
    wh, wm, _ = _split3(rw_ref[...])
    logits = (jnp.dot(xh, wh, preferred_element_type=F32) + jnp.dot(xh, wm, preferred_element_type=F32)
              + jnp.dot(xm, wh, preferred_element_type=F32)) + rb_ref[...]
    lane = _iota((tm, LANES), 1)
    lane_f = lane.astype(F32)
    cur = jnp.where(lane < N_EXPERTS, logits, -jnp.inf)
    vals, idxs, hots = [], [], []
    for _k in range(TOP_K):
        m = jnp.max(cur, -1, keepdims=True)
        ik = jnp.min(jnp.where(cur == m, lane_f, float(LANES)), -1, keepdims=True)
        hot = lane_f == ik
        vals.append(m)
        idxs.append(ik)
        hots.append(hot)
        cur = jnp.where(hot, -jnp.inf, cur)
    exps = [jnp.exp(vk - vals[0]) for vk in vals]
    denom = exps[0] + exps[1] + exps[2] + exps[3]
    count = jnp.zeros((tm, LANES), F32)
    for hot in hots:
        count = count + jnp.where(hot, 1.0, 0.0)
    strict = jnp.where(_iota((tm, tm), 1) < _iota((tm, tm), 0), 1.0, 0.0).astype(BF16)
    before = jnp.dot(strict, count.astype(BF16), preferred_element_type=F32) + carry_ref[...]
    out = jnp.zeros((tm, LANES), F32)
    for kk in range(TOP_K):
        rank = jnp.sum(jnp.where(hots[kk], before, 0.0), -1, keepdims=True)
        out = jnp.where(lane == ROUTE_IDX + kk, idxs[kk], out)
        out = jnp.where(lane == ROUTE_RANK + kk, rank, out)
        out = jnp.where(lane == ROUTE_GATE + kk, exps[kk] / denom, out)
    out_ref[...] = out
    total = carry_ref[...] + jnp.sum(count, 0, keepdims=True)
    carry_ref[...] = total
    cnt_ref[...] = total


def _route(x1, rw, rb):
    n = x1.shape[0]
    tm = TOK_TILE
    return pl.pallas_call(
        _route_kernel,
        grid=(n // tm,),
        in_specs=[pl.BlockSpec((tm, D_MODEL), lambda i: (i, 0)),
                  pl.BlockSpec((D_MODEL, LANES), lambda i: (0, 0)),
                  pl.BlockSpec((1, LANES), lambda i: (0, 0))],
        out_specs=[pl.BlockSpec((tm, LANES), lambda i: (i, 0)), pl.BlockSpec((1, LANES), lambda i: (0, 0))],
        out_shape=[jax.ShapeDtypeStruct((n, LANES), F32), jax.ShapeDtypeStruct((1, LANES), F32)],
        scratch_shapes=[pltpu.VMEM((1, LANES), F32)],
        compiler_params=_cparams(("arbitrary",)),
        name="router",
    )(x1, rw, rb)


def _row_copy(src, dst, sem):
    return pltpu.make_async_copy(src, dst, sem)


def _scatter_kernel(dest_ref, x_ref, xs_in_ref, xs_ref, sem):
    del xs_in_ref
    tm = x_ref.shape[0]

    def start(n, carry):
        for kk in range(TOP_K):
            d = dest_ref[n * TOP_K + kk]
            _row_copy(x_ref.at[pl.ds(n, 1), :], xs_ref.at[pl.ds(d, 1), :], sem).start(priority=kk % 2)
        return carry

    lax.fori_loop(0, tm, start, 0, unroll=DMA_UNROLL)

    def wait(n, carry):
        for kk in range(TOP_K):
            _row_copy(x_ref.at[pl.ds(0, 1), :], xs_ref.at[pl.ds(0, 1), :], sem).wait()
        return carry

    lax.fori_loop(0, tm, wait, 0, unroll=DMA_UNROLL)


def _scatter(dest, x1, n_slots):
    n = x1.shape[0]
    tm = TOK_TILE
    zeros = jnp.zeros((n_slots, D_MODEL), F32)
    return pl.pallas_call(
        _scatter_kernel,
        grid=(n // tm,),
        in_specs=[pl.BlockSpec((tm * TOP_K,), lambda i: (i,), memory_space=pltpu.SMEM),
                  pl.BlockSpec((tm, D_MODEL), lambda i: (i, 0)),
                  pl.BlockSpec(memory_space=pl.ANY)],
        out_specs=pl.BlockSpec(memory_space=pl.ANY),
        out_shape=jax.ShapeDtypeStruct((n_slots, D_MODEL), F32),
        scratch_shapes=[pltpu.SemaphoreType.DMA(())],
        input_output_aliases={2: 0},
        compiler_params=_cparams(("arbitrary",)),
        name="moe_scatter",
    )(dest, x1, zeros)


def _ffn_kernel(be_ref, nu_ref, xs_ref, wgu_ref, bgu_ref, wd_ref, bd_ref, ys_ref, wgu_b, wd_b):
    i = pl.program_id(0)
    live = i < nu_ref[0]
    fresh = jnp.logical_or(i == 0, be_ref[i] != be_ref[jnp.maximum(i - 1, 0)])

    @pl.when(jnp.logical_and(live, fresh))
    def _():
        wgu_b[...] = wgu_ref[0, 0].astype(BF16)
        wd_b[...] = wd_ref[0, 0].astype(BF16)

    @pl.when(live)
    def _():
        gu = jnp.dot(xs_ref[...].astype(BF16), wgu_b[...], preferred_element_type=F32) + bgu_ref[0, 0]
        gate = jnp.minimum(gu[:, 0:D_EXPERT], SWIGLU_LIMIT)
        up = jnp.clip(gu[:, D_EXPERT:], -SWIGLU_LIMIT, SWIGLU_LIMIT)
        glu = gate * _sigmoid(SWIGLU_ALPHA * gate)
        act = ((up + 1.0) * glu).astype(BF16)
        ys_ref[...] = jnp.dot(act, wd_b[...], preferred_element_type=F32) + bd_ref[0, 0]

    @pl.when(jnp.logical_not(live))
    def _():
        ys_ref[...] = jnp.zeros_like(ys_ref)


def _ffn(layer, block_e, n_used, xs, w_gu, b_gu, w_down, b_down):
    n_slots = xs.shape[0]
    bm = EXPERT_ROWS
    grid_spec = pltpu.PrefetchScalarGridSpec(
        num_scalar_prefetch=2,
        grid=(n_slots // bm,),
        in_specs=[pl.BlockSpec((bm, D_MODEL), lambda i, be, nu: (i, 0)),
                  pl.BlockSpec((1, 1, D_MODEL, 2 * D_EXPERT), lambda i, be, nu: (layer, be[i], 0, 0)),
                  pl.BlockSpec((1, 1, 1, 2 * D_EXPERT), lambda i, be, nu: (layer, be[i], 0, 0)),
                  pl.BlockSpec((1, 1, D_EXPERT, D_MODEL), lambda i, be, nu: (layer, be[i], 0, 0)),
                  pl.BlockSpec((1, 1, 1, D_MODEL), lambda i, be, nu: (layer, be[i], 0, 0))],
        out_specs=pl.BlockSpec((bm, D_MODEL), lambda i, be, nu: (i, 0)),
        scratch_shapes=[pltpu.VMEM((D_MODEL, 2 * D_EXPERT), BF16), pltpu.VMEM((D_EXPERT, D_MODEL), BF16)],
    )
    return pl.pallas_call(
        _ffn_kernel,
        grid_spec=grid_spec,
        out_shape=jax.ShapeDtypeStruct((n_slots, D_MODEL), F32),
        compiler_params=pltpu.CompilerParams(dimension_semantics=("arbitrary",),
                                             vmem_limit_bytes=FFN_VMEM_LIMIT),
        name="moe_ffn",
    )(block_e, n_used, xs, w_gu, b_gu, w_down, b_down)


def _combine_kernel(dest_ref, base_ref, route_ref, ys_ref, g2_ref, b2_ref, xo_ref, xb_ref, buf, sem):
    tm = base_ref.shape[0]

    def start(n, carry):
        for kk in range(TOP_K):
            d = dest_ref[n * TOP_K + kk]
            _row_copy(ys_ref.at[pl.ds(d, 1), :], buf.at[kk, pl.ds(n, 1), :], sem).start(priority=kk % 2)
        return carry

    lax.fori_loop(0, tm, start, 0, unroll=DMA_UNROLL)

    def wait(n, carry):
        for kk in range(TOP_K):
            _row_copy(ys_ref.at[pl.ds(0, 1), :], buf.at[0, pl.ds(0, 1), :], sem).wait()
        return carry

    lax.fori_loop(0, tm, wait, 0, unroll=DMA_UNROLL)

    acc = base_ref[...]
    for kk in range(TOP_K):
        acc = acc + route_ref[:, ROUTE_GATE + kk:ROUTE_GATE + kk + 1] * buf[kk]
    out = _layernorm_rows(acc, g2_ref[...], b2_ref[...])
    xo_ref[...] = out
    xb_ref[...] = out.astype(BF16)


def _combine(dest, base, route, ys, g2, b2):
    n = base.shape[0]
    tm = TOK_TILE
    tok = lambda width: pl.BlockSpec((tm, width), lambda i: (i, 0))
    return pl.pallas_call(
        _combine_kernel,
        grid=(n // tm,),
        in_specs=[pl.BlockSpec((tm * TOP_K,), lambda i: (i,), memory_space=pltpu.SMEM),
                  tok(D_MODEL), tok(LANES), pl.BlockSpec(memory_space=pl.ANY),
                  pl.BlockSpec((1, D_MODEL), lambda i: (0, 0)), pl.BlockSpec((1, D_MODEL), lambda i: (0, 0))],
        out_specs=[tok(D_MODEL), tok(D_MODEL)],
        out_shape=[jax.ShapeDtypeStruct((n, D_MODEL), F32), jax.ShapeDtypeStruct((n, D_MODEL), BF16)],
        scratch_shapes=[pltpu.VMEM((TOP_K, tm, D_MODEL), F32), pltpu.SemaphoreType.DMA(())],
        compiler_params=_cparams(("arbitrary",)),
        name="moe_combine_ln2",
    )(dest, base, route, ys, g2, b2)


def _row(v):
    return v.reshape(1, -1).astype(F32)


def _pad_rows(m, rows, offset):
    out = jnp.zeros((rows, m.shape[1]), m.dtype)
    return lax.dynamic_update_slice(out, m, (offset, 0))


def _regroup_w_in(w_in):
    d = w_in.shape[0]
    za = w_in[:, 0:A_MAIN]
    zb = w_in[:, 1792:3840]
    zc = w_in[:, 3840:5888]
    igfg = w_in[:, 5888:5896]
    zd = w_in[:, 5896:6920]
    zg = w_in[:, 6920:11016]
    vr = w_in[:, 11016:] if w_in.shape[1] > 11016 else jnp.zeros((d, 32), w_in.dtype)
    small = jnp.concatenate([vr, igfg, jnp.zeros((d, LANES - 40), w_in.dtype)], axis=1)
    a_grp = jnp.concatenate([za, small, jnp.zeros((d, 2048 - A_MAIN - LANES), w_in.dtype)], axis=1)
    return jnp.concatenate([zg, zb, zc, a_grp, zd], axis=1).astype(BF16)


def _block_diag(w):
    g, di, dj = w.shape
    eye = jnp.eye(g, dtype=w.dtype)
    return (eye[:, None, :, None] * w[:, :, None, :]).reshape(g * di, g * dj)


def kernel(x, p, w_in_first, w_in_rest, rwkv_mu, rwkv_w0, rwkv_w2, rwkv_a0, rwkv_a2, rwkv_v0, rwkv_v2,
           rwkv_g2, rwkv_k_k, rwkv_k_a, rwkv_r_k, rwkv_lnx_g, rwkv_lnx_b, hgrn_lower_bounds, hgrn_norm_w,
           mlstm_conv_w, mlstm_conv_b, mlstm_i_bias, mlstm_f_bias, mlstm_norm_w, lru_conv_w, lru_conv_b,
           lru_wx, lru_bx, lru_wa, lru_ba, lru_lambda, w_branch, w_out, ln1_g, ln1_b, router_w, router_b,
           expert_w_gu, expert_b_gu, expert_w_down, expert_b_down, ple_gate_w, ple_proj_w, ln2_g, ln2_b):
    bsz, seq, _ = x.shape
    n = bsz * seq
    n_assign = n * TOP_K
    n_blocks = n_assign // EXPERT_ROWS + N_EXPERTS
    n_slots = n_blocks * EXPERT_ROWS

    lb_all = jnp.cumsum(jax.nn.softmax(hgrn_lower_bounds.astype(F32), axis=0), axis=0)
    lb_all = lb_all - lb_all[0]

    xf = x.reshape(n, D_MODEL).astype(F32)
    xb = xf.astype(BF16)
    v_first = jnp.zeros((n, W), F32)
    for layer in range(DEPTH):
        first = layer == 0
        w_in = _regroup_w_in(w_in_first if first else w_in_rest[layer - 1])
        z = _proj(xb, w_in)

        rw_prm = {
            "mu": _row(rwkv_mu[layer]),
            "w0": _row(rwkv_w0[layer]),
            "w2": _pad_rows(rwkv_w2[layer], LANES, 0).astype(BF16),
            "a0": _row(rwkv_a0[layer]),
            "a2": _pad_rows(rwkv_a2[layer], LANES, 64).astype(BF16),
            "v0": _row(jnp.zeros((W,), F32) if first else rwkv_v0[layer - 1]),
            "v2": (jnp.zeros((LANES, W), BF16) if first
                   else _pad_rows(rwkv_v2[layer - 1], LANES, SMALL_VR).astype(BF16)),
            "g2": rwkv_g2[layer].astype(BF16),
            "k_k": _row(rwkv_k_k[layer]),
            "k_a": _row(rwkv_k_a[layer]),
            "r_k": _row(rwkv_r_k[layer]),
            "lnx_g": _row(rwkv_lnx_g[layer]),
            "lnx_b": _row(rwkv_lnx_b[layer]),
        }
        y_a, v_cur = _rwkv(z, v_first, rw_prm, bsz, seq, first)
        if first:
            v_first = v_cur
        y_b = _hgrn(z, _row(lb_all[layer]), _row(hgrn_norm_w[layer]), bsz, seq)
        gate_bias = jnp.zeros((LANES,), F32)
        gate_bias = gate_bias.at[SMALL_IG:SMALL_IG + MLSTM_HEADS].set(mlstm_i_bias[layer])
        gate_bias = gate_bias.at[SMALL_FG:SMALL_FG + MLSTM_HEADS].set(mlstm_f_bias[layer])
        y_c = _mlstm(z, mlstm_conv_w[layer].astype(F32), _row(mlstm_conv_b[layer]), _row(gate_bias),
                     _row(mlstm_norm_w[layer]), bsz, seq)
        y_d = _lru(z, lru_conv_w[layer].astype(F32), _row(lru_conv_b[layer]),
                   _block_diag(lru_wx[layer]).astype(BF16), _row(lru_bx[layer]),
                   _block_diag(lru_wa[layer]).astype(BF16), _row(lru_ba[layer]), _row(lru_lambda[layer]),
                   bsz, seq)

        x1, base = _merge(xf, (y_a, y_b, y_c, y_d), z, w_branch[layer].astype(BF16), w_out[layer].astype(BF16),
                          _row(ln1_g[layer]), _row(ln1_b[layer]), p[layer].reshape(n, D_PLE),
                          ple_gate_w[layer].astype(BF16), ple_proj_w[layer].astype(BF16))

        rw = jnp.zeros((D_MODEL, LANES), F32).at[:, 0:N_EXPERTS].set(router_w[layer])
        rb = jnp.zeros((1, LANES), F32).at[0, 0:N_EXPERTS].set(router_b[layer])
        route, counts_f = _route(x1, rw, rb)

        idx = route[:, ROUTE_IDX:ROUTE_IDX + TOP_K].astype(I32)
        rank = route[:, ROUTE_RANK:ROUTE_RANK + TOP_K].astype(I32)
        counts = counts_f[0, 0:N_EXPERTS].astype(I32)
        padded = (counts + EXPERT_ROWS - 1) // EXPERT_ROWS * EXPERT_ROWS
        pad_end = jnp.cumsum(padded)
        pad_start = pad_end - padded
        dest = (pad_start[idx] + rank).reshape(-1)
        block_start = jnp.arange(n_blocks, dtype=I32) * EXPERT_ROWS
        block_e = jnp.minimum(jnp.sum((pad_end[None, :] <= block_start[:, None]).astype(I32), axis=1),
                              N_EXPERTS - 1)
        n_used = (pad_end[-1:] // EXPERT_ROWS).astype(I32)

        xs = _scatter(dest, x1, n_slots)
        ys = _ffn(layer, block_e, n_used, xs, expert_w_gu.astype(F32),
                  expert_b_gu.reshape(DEPTH, N_EXPERTS, 1, 2 * D_EXPERT).astype(F32),
                  expert_w_down.astype(F32),
                  expert_b_down.reshape(DEPTH, N_EXPERTS, 1, D_MODEL).astype(F32))
        xf, xb = _combine(dest, base, route, ys, _row(ln2_g[layer]), _row(ln2_b[layer]))
    return xf.reshape(bsz, seq, D_MODEL).astype(x.dtype)
```

```python
import functools
import math

import jax
import jax.numpy as jnp
from jax import lax
from jax.experimental import pallas as pl
from jax.experimental.pallas import tpu as pltpu

F32 = jnp.float32
BF16 = jnp.bfloat16
I32 = jnp.int32

D_MODEL = 1024
DEPTH = 4
W = 512
RWKV_HEAD_DIM = 64
RWKV_LNX_EPS = 64e-5
MLSTM_HEADS = 4
HEAD128 = 128
CHUNK = 64
LRU_C = 8.0
CONV_WIDTH = 4
N_EXPERTS = 32
TOP_K = 4
D_EXPERT = 1024
SWIGLU_LIMIT = 7.0
SWIGLU_ALPHA = 1.702
D_PLE = 256
LN_EPS = 1e-5
NORM_EPS = 1e-6
ALPHA = (2 * DEPTH) ** 0.25

COL_G = 0
COL_B = 4096
COL_C = 6144
COL_A = 8192
COL_D = 10240
N_COLS = 11264
A_MAIN = 1792
SMALL_VR = 0
SMALL_IG = 32
SMALL_FG = 36

EXPERT_ROWS = 512
TOK_TILE = 256
LANES = 128
DMA_UNROLL = 8
VMEM_LIMIT = 48 * 1024 * 1024
FFN_VMEM_LIMIT = 56 * 1024 * 1024


def _cparams(sem):
    return pltpu.CompilerParams(dimension_semantics=sem, vmem_limit_bytes=VMEM_LIMIT)


def _mm(a, b):
    return jnp.dot(a.astype(BF16), b.astype(BF16), preferred_element_type=F32)


def _mm_nt(a, b):
    return lax.dot_general(a.astype(BF16), b.astype(BF16), (((1,), (1,)), ((), ())),
                           preferred_element_type=F32)


def _mm_tn(a, b):
    return jnp.dot(a.T.astype(BF16), b.astype(BF16), preferred_element_type=F32)


def _split3(x):
    hi = x.astype(BF16)
    r1 = x - hi.astype(F32)
    mid = r1.astype(BF16)
    lo = (r1 - mid.astype(F32)).astype(BF16)
    return hi, mid, lo


def _mm_exact_lhs(m, x):
    hi, mid, lo = _split3(x)
    return (jnp.dot(m, hi, preferred_element_type=F32) + jnp.dot(m, mid, preferred_element_type=F32)
            + jnp.dot(m, lo, preferred_element_type=F32))


def _mm_exact_rhs(x, m):
    hi, mid, lo = _split3(x)
    return (jnp.dot(hi, m, preferred_element_type=F32) + jnp.dot(mid, m, preferred_element_type=F32)
            + jnp.dot(lo, m, preferred_element_type=F32))


def _iota(shape, dim):
    return lax.broadcasted_iota(I32, shape, dim)


def _tri_incl(n):
    return jnp.where(_iota((n, n), 1) <= _iota((n, n), 0), 1.0, 0.0).astype(BF16)


def _seg_ones(n, seg):
    return jnp.where(_iota((n, n), 0) // seg == _iota((n, n), 1) // seg, 1.0, 0.0).astype(BF16)


def _sigmoid(x):
    return jax.nn.sigmoid(x)


def _silu(x):
    return x * jax.nn.sigmoid(x)


def _log_sigmoid(x):
    return jnp.minimum(x, 0.0) - jnp.log1p(jnp.exp(-jnp.abs(x)))


U32 = jnp.uint32
HALF = D_MODEL // 2


def _pack_rows(x):
    bits = lax.bitcast_convert_type(x.astype(BF16).astype(F32), U32)
    return (bits[:, 0:HALF] >> 16) | bits[:, HALF:]


def _unpack_rows(w):
    lo = lax.bitcast_convert_type(w << 16, F32)
    hi = lax.bitcast_convert_type(w & jnp.uint32(0xFFFF0000), F32)
    return jnp.concatenate([lo, hi], axis=1)


def _layernorm_rows(x, g, b):
    xc = x - jnp.mean(x, -1, keepdims=True)
    var = jnp.mean(xc * xc, -1, keepdims=True)
    return xc * lax.rsqrt(var + LN_EPS) * g + b


def _proj_kernel(x_ref, w_ref, o_ref):
    o_ref[...] = jnp.dot(x_ref[...], w_ref[...], preferred_element_type=F32).astype(o_ref.dtype)


def _proj(xb, w):
    n, k = xb.shape
    c = w.shape[1]
    tm = min(2048, n)
    tn = 1024
    return pl.pallas_call(
        _proj_kernel,
        grid=(n // tm, c // tn),
        in_specs=[pl.BlockSpec((tm, k), lambda i, j: (i, 0)),
                  pl.BlockSpec((k, tn), lambda i, j: (0, j))],
        out_specs=pl.BlockSpec((tm, tn), lambda i, j: (i, j)),
        out_shape=jax.ShapeDtypeStruct((n, c), BF16),
        compiler_params=_cparams(("parallel", "parallel")),
        name="in_proj",
    )(xb, w)


RWKV_CHUNKS = 4
RWKV_T = CHUNK * RWKV_CHUNKS


def _rwkv_kernel(first, z_ref, vf_ref, mu_ref, w0_ref, w2_ref, a0_ref, a2_ref, v0_ref, v2_ref, g2_ref,
                 kk_ref, ka_ref, rk_ref, lng_ref, lnb_ref, y_ref, vo_ref, prev_ref, s_ref):
    t_len = RWKV_T

    @pl.when(pl.program_id(1) == 0)
    def _():
        prev_ref[...] = jnp.zeros_like(prev_ref)
        s_ref[...] = jnp.zeros_like(s_ref)

    z = z_ref[:, 0:A_MAIN].astype(F32)
    small = z_ref[:, A_MAIN:A_MAIN + LANES].astype(F32)
    row = _iota((t_len, 1), 0)
    zs = jnp.where(row == 0, prev_ref[...], pltpu.roll(z, 1, 0))
    prev_ref[...] = z[t_len - 1:t_len, :]
    zz = z + (zs - z) * mu_ref[...]

    r = zz[:, 0:W]
    k = zz[:, W:2 * W]
    v = zz[:, 2 * W:3 * W]
    xwa = zz[:, 3 * W:3 * W + LANES]
    xg = zz[:, 3 * W + LANES:3 * W + 2 * LANES]

    u = w0_ref[...] + _mm(jnp.tanh(xwa), w2_ref[...])
    logw = -math.exp(-0.5) * _sigmoid(u)
    a_sig = _sigmoid(a0_ref[...] + _mm(xwa, a2_ref[...]))
    g = _mm(_sigmoid(xg), g2_ref[...])
    if not first:
        v = v + (vf_ref[...] - v) * _sigmoid(v0_ref[...] + _mm(small, v2_ref[...]))
    vo_ref[...] = v

    seg = _seg_ones(LANES, RWKV_HEAD_DIM)
    tri = jnp.where((_iota((t_len, t_len), 1) <= _iota((t_len, t_len), 0))
                    & (_iota((t_len, t_len), 1) // CHUNK == _iota((t_len, t_len), 0) // CHUNK),
                    1.0, 0.0).astype(BF16)
    c = _mm_exact_lhs(tri, logw)
    gam = jnp.exp(c)
    igam = jnp.exp(-c)
    gam_ex = jnp.exp(c - logw)

    lane = _iota((1, LANES), 1)
    lane2 = _iota((1, 2 * LANES), 1)
    hm = [lane < RWKV_HEAD_DIM, lane >= RWKV_HEAD_DIM]
    hm2 = [lane2 % LANES < RWKV_HEAD_DIM, lane2 % LANES >= RWKV_HEAD_DIM]
    bd_mask = _iota((LANES, LANES), 0) // RWKV_HEAD_DIM == _iota((LANES, LANES), 1) // RWKV_HEAD_DIM
    pstrict = (_iota((CHUNK, LANES), 1) % RWKV_HEAD_DIM) < _iota((CHUNK, LANES), 0)
    pincl = (_iota((CHUNK, LANES), 1) % RWKV_HEAD_DIM) <= _iota((CHUNK, LANES), 0)

    def bdiag(m, masks=hm):
        return jnp.concatenate([jnp.where(masks[0], m, 0.0), jnp.where(masks[1], m, 0.0)], axis=0)

    n_pair = W // LANES
    sls = [slice(p * LANES, (p + 1) * LANES) for p in range(n_pair)]
    rss = [slice(ci * CHUNK, (ci + 1) * CHUNK) for ci in range(RWKV_CHUNKS)]
    kk_raw = [k[:, sl] * kk_ref[:, sl] for sl in sls]
    kkn = [kr / jnp.maximum(jnp.sqrt(_mm_exact_rhs(kr * kr, seg)), 1e-12) for kr in kk_raw]
    aps = [a_sig[:, sl] for sl in sls]
    k2s = [k[:, sl] * (1.0 + (ap - 1.0) * ka_ref[:, sl]) for sl, ap in zip(sls, aps)]
    r_ts = [r[:, sl] * gam[:, sl] for sl in sls]
    a_ts = [-kn * gam_ex[:, sl] for kn, sl in zip(kkn, sls)]
    kb = [kn * ap for kn, ap in zip(kkn, aps)]
    b_ts = [x_ * igam[:, sl] for x_, sl in zip(kb, sls)]
    k_ts = [k2 * igam[:, sl] for k2, sl in zip(k2s, sls)]

    cps = [(ci, p) for ci in range(RWKV_CHUNKS) for p in range(n_pair)]
    lhs = [jnp.concatenate([a_ts[p][rss[ci]], r_ts[p][rss[ci]]], axis=0) for ci, p in cps]
    pb = [_mm_nt(lhs[i], bdiag(b_ts[p][rss[ci]])) for i, (ci, p) in enumerate(cps)]
    pk = [_mm_nt(lhs[i], bdiag(k_ts[p][rss[ci]])) for i, (ci, p) in enumerate(cps)]
    v_bd = [bdiag(v[rss[ci], sls[p]]) for ci, p in cps]
    a_ak = [jnp.where(pstrict, m[0:CHUNK], 0.0) for m in pk]
    a_rb = [jnp.where(pincl, m[CHUNK:], 0.0) for m in pb]
    a_rk = [jnp.where(pincl, m[CHUNK:], 0.0) for m in pk]
    pw = [jnp.where(pstrict, m[0:CHUNK], 0.0) for m in pb]
    w2 = [jnp.concatenate([_mm(a_ak[i], v_bd[i]), a_ts[p][rss[ci]]], axis=1) for i, (ci, p) in enumerate(cps)]
    for it in range(6):
        w2 = [x_ + _mm(q_, bdiag(x_, hm2)) for x_, q_ in zip(w2, pw)]
        if it < 5:
            pw = [_mm(q_, bdiag(q_)) for q_ in pw]

    s_cur = [s_ref[p] for p in range(n_pair)]
    y_rows = [[] for _ in range(n_pair)]
    for ci in range(RWKV_CHUNKS):
        rs = rss[ci]
        idx = [ci * n_pair + p for p in range(n_pair)]
        c_last = c[ci * CHUNK + CHUNK - 1:ci * CHUNK + CHUNK, :]
        tail = jnp.exp(c_last - c[rs])
        g_last = jnp.exp(c_last)
        ls0 = [_mm_nt(jnp.concatenate([w2[idx[p]][:, LANES:], r_ts[p][rs]], axis=0), s_cur[p])
               for p in range(n_pair)]
        xs = [w2[idx[p]][:, 0:LANES] + ls0[p][0:CHUNK] for p in range(n_pair)]
        ys = [ls0[p][CHUNK:] + _mm(jnp.concatenate([a_rb[idx[p]], a_rk[idx[p]]], axis=1),
                                   jnp.concatenate([bdiag(xs[p]), v_bd[idx[p]]], axis=0))
              for p in range(n_pair)]
        upd = [_mm_tn(jnp.concatenate([xs[p], v[rs, sls[p]]], axis=0),
                      jnp.concatenate([kb[p][rs] * tail[:, sls[p]], k2s[p][rs] * tail[:, sls[p]]], axis=0))
               for p in range(n_pair)]
        s_cur = [s_cur[p] * g_last[:, sls[p]] + jnp.where(bd_mask, upd[p], 0.0) for p in range(n_pair)]
        for p in range(n_pair):
            y_rows[p].append(ys[p])
    for p in range(n_pair):
        s_ref[p] = s_cur[p]

    y_pairs = [jnp.concatenate(rows_, axis=0) for rows_ in y_rows]
    means = [_mm_exact_rhs(y_, seg) * (1.0 / RWKV_HEAD_DIM) for y_ in y_pairs]
    ycs = [y_ - m_ for y_, m_ in zip(y_pairs, means)]
    vars_ = [_mm_exact_rhs(yc * yc, seg) * (1.0 / RWKV_HEAD_DIM) for yc in ycs]
    bonus = [_mm_exact_rhs(r[:, sls[p]] * k2s[p] * rk_ref[:, sls[p]], seg) * v[:, sls[p]] for p in range(n_pair)]
    for p in range(n_pair):
        sl = sls[p]
        yn = ycs[p] * lax.rsqrt(vars_[p] + RWKV_LNX_EPS) * lng_ref[:, sl] + lnb_ref[:, sl]
        y_ref[:, sl] = ((yn + bonus[p]) * g[:, sl]).astype(y_ref.dtype)


def _rwkv(z, v_first, prm, bsz, seq, first):
    n = bsz * seq
    nj = seq // RWKV_T
    row_spec = lambda width: pl.BlockSpec((1, width), lambda b, j: (0, 0))
    mat_spec = lambda r_, c_: pl.BlockSpec((r_, c_), lambda b, j: (0, 0))
    tok = lambda width: pl.BlockSpec((RWKV_T, width), lambda b, j: (b * nj + j, 0))
    in_specs = [
        pl.BlockSpec((RWKV_T, 2048), lambda b, j: (b * nj + j, COL_A // 2048)),
        tok(W),
        row_spec(A_MAIN), row_spec(W), mat_spec(LANES, W), row_spec(W), mat_spec(LANES, W),
        row_spec(W), mat_spec(LANES, W), mat_spec(LANES, W),
        row_spec(W), row_spec(W), row_spec(W), row_spec(W), row_spec(W),
    ]
    return pl.pallas_call(
        functools.partial(_rwkv_kernel, first),
        grid=(bsz, nj),
        in_specs=in_specs,
        out_specs=[tok(W), tok(W)],
        out_shape=[jax.ShapeDtypeStruct((n, W), BF16), jax.ShapeDtypeStruct((n, W), F32)],
        scratch_shapes=[pltpu.VMEM((1, A_MAIN), F32), pltpu.VMEM((W // LANES, LANES, LANES), F32)],
        compiler_params=_cparams(("parallel", "arbitrary")),
        name="rwkv7",
    )(z, v_first, prm["mu"], prm["w0"], prm["w2"], prm["a0"], prm["a2"], prm["v0"], prm["v2"], prm["g2"],
      prm["k_k"], prm["k_a"], prm["r_k"], prm["lnx_g"], prm["lnx_b"])


HG_SUB = 16


def _hgrn_kernel(z_ref, lb_ref, nw_ref, y_ref, s_ref, q_s, k_s, v_s, bc_s, o_s):
    t_len = CHUNK

    @pl.when(pl.program_id(1) == 0)
    def _():
        s_ref[...] = jnp.zeros_like(s_ref)

    zq = z_ref[:, 0:W].astype(F32)
    f = z_ref[:, W:2 * W].astype(F32)
    lb = lb_ref[...]
    q_s[...] = _silu(zq)
    v_s[...] = z_ref[:, 2 * W:3 * W].astype(F32)
    la = jnp.log(lb)
    lbb = jnp.log1p(-lb) + _log_sigmoid(f)
    logf = jnp.maximum(la, lbb) + jnp.log1p(jnp.exp(-jnp.abs(la - lbb)))
    k_s[...] = (1.0 - lb) * _sigmoid(-f)
    blk = jnp.where((_iota((t_len, t_len), 1) <= _iota((t_len, t_len), 0))
                    & (_iota((t_len, t_len), 1) // HG_SUB == _iota((t_len, t_len), 0) // HG_SUB),
                    1.0, 0.0).astype(BF16)
    bc_s[...] = _mm_exact_lhs(blk, logf)

    row8 = _iota((HG_SUB // 2, 1), 0)

    def sub_block(i):
        r0 = i * HG_SUB
        n_head = W // HEAD128
        sls = [slice(h * HEAD128, (h + 1) * HEAD128) for h in range(n_head)]
        qb = q_s[pl.ds(r0, HG_SUB), :]
        kb = k_s[pl.ds(r0, HG_SUB), :]
        vb = v_s[pl.ds(r0, HG_SUB), :]
        bcb = bc_s[pl.ds(r0, HG_SUB), :]
        st = [s_ref[h] for h in range(n_head)]
        qd = qb * jnp.exp(bcb)
        o = [_mm_nt(qd[:, sl], st[h]) for h, sl in enumerate(sls)]
        half = HG_SUB // 2
        o_top = [x_[0:half] for x_ in o]
        o_bot = [x_[half:] for x_ in o]
        for s in range(HG_SUB):
            bs = bcb[s:s + 1, :]
            ks = kb[s:s + 1, :]
            if s < half:
                p_top = qb[0:half] * ks * jnp.exp(jnp.where(row8 >= s, bcb[0:half] - bs, -jnp.inf))
                p_bot = qb[half:] * ks * jnp.exp(bcb[half:] - bs)
            else:
                p_top = None
                p_bot = qb[half:] * ks * jnp.exp(jnp.where(row8 >= s - half, bcb[half:] - bs, -jnp.inf))
            for h, sl in enumerate(sls):
                vs = vb[s:s + 1, sl]
                if p_top is not None:
                    o_top[h] = o_top[h] + jnp.sum(p_top[:, sl], axis=-1, keepdims=True) * vs
                o_bot[h] = o_bot[h] + jnp.sum(p_bot[:, sl], axis=-1, keepdims=True) * vs
        bl = bcb[HG_SUB - 1:HG_SUB, :]
        kd = kb * jnp.exp(bl - bcb)
        e_last = jnp.exp(bl)
        upd = [_mm_tn(vb[:, sl], kd[:, sl]) for sl in sls]
        for h, sl in enumerate(sls):
            s_ref[h] = st[h] * e_last[:, sl] + upd[h]
            o_s[pl.ds(r0, half), sl] = o_top[h]
            o_s[pl.ds(r0 + half, half), sl] = o_bot[h]

    for i in range(t_len // HG_SUB):
        sub_block(i)

    g = z_ref[:, 3 * W:4 * W].astype(F32)
    for h in range(W // HEAD128):
        sl = slice(h * HEAD128, (h + 1) * HEAD128)
        o = o_s[:, sl]
        o = o * lax.rsqrt(jnp.mean(o * o, -1, keepdims=True) + NORM_EPS) * nw_ref[:, sl]
        y_ref[:, sl] = (o * _silu(g[:, sl])).astype(y_ref.dtype)


def _hgrn(z, lb, norm_w, bsz, seq):
    n = bsz * seq
    nj = seq // CHUNK
    row_spec = pl.BlockSpec((1, W), lambda b, j: (0, 0))
    return pl.pallas_call(
        _hgrn_kernel,
        grid=(bsz, nj),
        in_specs=[pl.BlockSpec((CHUNK, 2048), lambda b, j: (b * nj + j, COL_B // 2048)), row_spec, row_spec],
        out_specs=pl.BlockSpec((CHUNK, W), lambda b, j: (b * nj + j, 0)),
        out_shape=jax.ShapeDtypeStruct((n, W), BF16),
        scratch_shapes=[pltpu.VMEM((W // HEAD128, HEAD128, HEAD128), F32)]
        + [pltpu.VMEM((CHUNK, W), F32) for _ in range(5)],
        compiler_params=_cparams(("parallel", "arbitrary")),
        name="hgrn2",
    )(z, lb, norm_w)


def _mlstm_kernel(z_ref, sm_ref, cw_ref, cb_ref, gb_ref, nw_ref, y_ref, ext_ref, c_ref, n_ref, m_ref):
    t_len = CHUNK
    pad = 8

    @pl.when(pl.program_id(1) == 0)
    def _():
        ext_ref[0:pad, :] = jnp.zeros((pad, 2 * W), F32)
        c_ref[...] = jnp.zeros_like(c_ref)
        n_ref[...] = jnp.zeros_like(n_ref)
        m_ref[...] = jnp.full(m_ref.shape, -1e30, F32)

    ext_ref[pad:pad + t_len, :] = z_ref[:, 0:2 * W].astype(F32)
    acc = cb_ref[...]
    for jj in range(CONV_WIDTH):
        acc = acc + ext_ref[pl.ds(pad - (CONV_WIDTH - 1) + jj, t_len), :] * cw_ref[jj:jj + 1, :]
    ext_ref[0:pad, :] = ext_ref[t_len:t_len + pad, :]
    qk = _silu(acc)
    q = qk[:, 0:W]
    k = qk[:, W:2 * W] * (HEAD128 ** -0.5)
    v = z_ref[:, 2 * W:3 * W].astype(F32)
    og = z_ref[:, 3 * W:4 * W].astype(F32)

    gates = sm_ref[...].astype(F32) + gb_ref[...]
    lf = _log_sigmoid(gates)
    fc = _mm_exact_lhs(_tri_incl(t_len), lf)
    gates_t = gates.T
    fc_t = fc.T
    causal = _iota((t_len, t_len), 1) <= _iota((t_len, t_len), 0)

    hs = range(MLSTM_HEADS)
    sls = [slice(h * HEAD128, (h + 1) * HEAD128) for h in hs]
    ic_col = [gates[:, SMALL_IG + h:SMALL_IG + h + 1] for h in hs]
    ic_row = [gates_t[SMALL_IG + h:SMALL_IG + h + 1, :] for h in hs]
    fc_col = [fc[:, SMALL_FG + h:SMALL_FG + h + 1] for h in hs]
    fc_row = [fc_t[SMALL_FG + h:SMALL_FG + h + 1, :] for h in hs]
    m_prev = [m_ref[h:h + 1, 0:1] for h in hs]
    c_mat = [c_ref[h] for h in hs]
    n_vec = [n_ref[h:h + 1, :] for h in hs]
    qk_raw = [_mm_nt(q[:, sl], k[:, sl]) for sl in sls]
    q_c = [_mm(q[:, sl], c_mat[h]) for h, sl in enumerate(sls)]
    log_w = [jnp.where(causal, fc_col[h] - fc_row[h] + ic_row[h], -jnp.inf) for h in hs]
    log_inter = [fc_col[h] + m_prev[h] for h in hs]
    m_t = [jnp.maximum(log_inter[h], jnp.max(log_w[h], -1, keepdims=True)) for h in hs]
    w_inter = [jnp.exp(log_inter[h] - m_t[h]) for h in hs]
    s_qk = [qk_raw[h] * jnp.exp(log_w[h] - m_t[h]) for h in hs]
    s_v = [_mm(s_qk[h], v[:, sl]) for h, sl in enumerate(sls)]
    f_last = [fc_col[h][t_len - 1:t_len, :] for h in hs]
    log_s = [f_last[h] - fc_col[h] + ic_col[h] for h in hs]
    m_new = [jnp.maximum(f_last[h] + m_prev[h], jnp.max(log_s[h], 0, keepdims=True)) for h in hs]
    kw = [k[:, sl] * jnp.exp(log_s[h] - m_new[h]) for h, sl in enumerate(sls)]
    kv = [_mm_tn(kw[h], v[:, sl]) for h, sl in enumerate(sls)]
    for h, sl in enumerate(sls):
        num = w_inter[h] * q_c[h] + s_v[h]
        den = (w_inter[h] * jnp.sum(q[:, sl] * n_vec[h], -1, keepdims=True)
               + jnp.sum(s_qk[h], -1, keepdims=True))
        hid = num / jnp.maximum(jnp.abs(den), jnp.exp(-m_t[h]))
        decay = jnp.exp(f_last[h] + m_prev[h] - m_new[h])
        c_ref[h] = decay * c_mat[h] + kv[h]
        n_ref[h:h + 1, :] = decay * n_vec[h] + jnp.sum(kw[h], 0, keepdims=True)
        m_ref[h:h + 1, :] = jnp.broadcast_to(m_new[h], (1, LANES))
        hc = hid - jnp.mean(hid, -1, keepdims=True)
        hn = hc * lax.rsqrt(jnp.mean(hc * hc, -1, keepdims=True) + NORM_EPS) * nw_ref[:, sl]
        y_ref[:, sl] = (_sigmoid(og[:, sl]) * hn).astype(y_ref.dtype)


def _mlstm(z, conv_w, conv_b, gate_bias, norm_w, bsz, seq):
    n = bsz * seq
    nj = seq // CHUNK
    const = lambda r_, c_: pl.BlockSpec((r_, c_), lambda b, j: (0, 0))
    return pl.pallas_call(
        _mlstm_kernel,
        grid=(bsz, nj),
        in_specs=[pl.BlockSpec((CHUNK, 2048), lambda b, j: (b * nj + j, COL_C // 2048)),
                  pl.BlockSpec((CHUNK, LANES), lambda b, j: (b * nj + j, (COL_A + A_MAIN) // LANES)),
                  const(CONV_WIDTH, 2 * W), const(1, 2 * W), const(1, LANES), const(1, W)],
        out_specs=pl.BlockSpec((CHUNK, W), lambda b, j: (b * nj + j, 0)),
        out_shape=jax.ShapeDtypeStruct((n, W), BF16),
        scratch_shapes=[pltpu.VMEM((CHUNK + 8, 2 * W), F32),
                        pltpu.VMEM((MLSTM_HEADS, HEAD128, HEAD128), F32),
                        pltpu.VMEM((8, HEAD128), F32), pltpu.VMEM((8, LANES), F32)],
        compiler_params=_cparams(("parallel", "arbitrary")),
        name="mlstm",
    )(z, z, conv_w, conv_b, gate_bias, norm_w)


LRU_T = 256


def _lru_kernel(z_ref, cw_ref, cb_ref, wx_ref, bx_ref, wa_ref, ba_ref, lam_ref, y_ref, ext_ref, h_ref):
    t_len = LRU_T
    pad = 8
    first_blk = pl.program_id(1) == 0

    @pl.when(first_blk)
    def _():
        ext_ref[0:pad, :] = jnp.zeros((pad, W), F32)
        h_ref[...] = jnp.zeros_like(h_ref)

    ext_ref[pad:pad + t_len, :] = z_ref[:, 0:W].astype(F32)
    xc = cb_ref[...]
    for jj in range(CONV_WIDTH):
        xc = xc + ext_ref[pl.ds(pad - (CONV_WIDTH - 1) + jj, t_len), :] * cw_ref[jj:jj + 1, :]
    ext_ref[0:pad, :] = ext_ref[t_len:t_len + pad, :]

    gate_x = _sigmoid(_mm(xc, wx_ref[...]) + bx_ref[...])
    gate_a = _sigmoid(_mm(xc, wa_ref[...]) + ba_ref[...])
    log_a = LRU_C * gate_a * _log_sigmoid(lam_ref[...])
    a = jnp.exp(log_a)
    mult = jnp.sqrt(-jnp.tanh(log_a) * (a * a + 1.0))
    row = _iota((t_len, 1), 0)
    mult = jnp.where(jnp.logical_and(first_blk, row == 0), 1.0, mult)
    b = mult * gate_x * xc

    d = 1
    while d < t_len:
        keep = row >= d
        a_sh = jnp.where(keep, pltpu.roll(a, d, 0), 1.0)
        b_sh = jnp.where(keep, pltpu.roll(b, d, 0), 0.0)
        b = a * b_sh + b
        a = a * a_sh
        d *= 2
    h = a * h_ref[...] + b
    h_ref[...] = h[t_len - 1:t_len, :]
    y_ref[...] = (h * jax.nn.gelu(z_ref[:, W:2 * W].astype(F32), approximate=True)).astype(y_ref.dtype)


def _lru(z, conv_w, conv_b, wx, bx, wa, ba, lam, bsz, seq):
    n = bsz * seq
    nj = seq // LRU_T
    const = lambda r_, c_: pl.BlockSpec((r_, c_), lambda b, j: (0, 0))
    return pl.pallas_call(
        _lru_kernel,
        grid=(bsz, nj),
        in_specs=[pl.BlockSpec((LRU_T, 2 * W), lambda b, j: (b * nj + j, COL_D // (2 * W))),
                  const(CONV_WIDTH, W), const(1, W), const(W, W), const(1, W), const(W, W), const(1, W),
                  const(1, W)],
        out_specs=pl.BlockSpec((LRU_T, W), lambda b, j: (b * nj + j, 0)),
        out_shape=jax.ShapeDtypeStruct((n, W), BF16),
        scratch_shapes=[pltpu.VMEM((LRU_T + 8, W), F32), pltpu.VMEM((1, W), F32)],
        compiler_params=_cparams(("parallel", "arbitrary")),
        name="rglru",
    )(z, conv_w, conv_b, wx, bx, wa, ba, lam)


def _merge_kernel(x_ref, ya_ref, yb_ref, yc_ref, yd_ref, zg_ref, wb_ref, wo_ref, g1_ref, b1_ref,
                  p_ref, wpg_ref, wpp_ref, x1_ref, base_ref, x1p_ref):
    merged = None
    for nb, y_ref in enumerate((ya_ref, yb_ref, yc_ref, yd_ref)):
        proj = jnp.dot(y_ref[...], wb_ref[nb], preferred_element_type=F32)
        term = _sigmoid(zg_ref[:, nb * D_MODEL:(nb + 1) * D_MODEL].astype(F32)) * proj
        merged = term if merged is None else merged + term
    mix = _mm(merged, wo_ref[...])
    x1 = _layernorm_rows(ALPHA * x_ref[...] + mix, g1_ref[...], b1_ref[...])
    x1_ref[...] = x1
    x1p_ref[...] = _pack_rows(x1)
    ple = _sigmoid(_mm(x1, wpg_ref[...])) * _mm(p_ref[...], wpp_ref[...])
    base_ref[...] = ALPHA * x1 + ple


def _merge(x, ys, z, w_branch, w_out, g1, b1, p_l, w_pg, w_pp):
    n = x.shape[0]
    tm = TOK_TILE
    tok = lambda width: pl.BlockSpec((tm, width), lambda i: (i, 0))
    const2 = lambda r_, c_: pl.BlockSpec((r_, c_), lambda i: (0, 0))
    return pl.pallas_call(
        _merge_kernel,
        grid=(n // tm,),
        in_specs=[tok(D_MODEL), tok(W), tok(W), tok(W), tok(W),
                  pl.BlockSpec((tm, 4 * D_MODEL), lambda i: (i, COL_G // (4 * D_MODEL))),
                  pl.BlockSpec((4, W, D_MODEL), lambda i: (0, 0, 0)), const2(D_MODEL, D_MODEL),
                  const2(1, D_MODEL), const2(1, D_MODEL), tok(D_PLE), const2(D_MODEL, D_MODEL),
                  const2(D_PLE, D_MODEL)],
        out_specs=[tok(D_MODEL), tok(D_MODEL), tok(HALF)],
        out_shape=[jax.ShapeDtypeStruct((n, D_MODEL), F32), jax.ShapeDtypeStruct((n, D_MODEL), F32),
                   jax.ShapeDtypeStruct((n, HALF), U32)],
        compiler_params=_cparams(("parallel",)),
        name="merge_ln1",
    )(x, *ys, z, w_branch, w_out, g1, b1, p_l, w_pg, w_pp)


ROUTE_IDX = 0
ROUTE_RANK = 4
ROUTE_GATE = 8


def _route_kernel(x_ref, rw_ref, rb_ref, out_ref, cnt_ref, carry_ref):
    tm = x_ref.shape[0]

    @pl.when(pl.program_id(0) == 0)
    def _():
        carry_ref[...] = jnp.zeros_like(carry_ref)

    xh, xm, _ = _split3(x_ref[...])
    wh, wm, _ = _split3(rw_ref[...])
    logits = (jnp.dot(xh, wh, preferred_element_type=F32) + jnp.dot(xh, wm, preferred_element_type=F32)
              + jnp.dot(xm, wh, preferred_element_type=F32)) + rb_ref[...]
    lane = _iota((tm, LANES), 1)
    lane_f = lane.astype(F32)
    cur = jnp.where(lane < N_EXPERTS, logits, -jnp.inf)
    vals, idxs, hots = [], [], []
    for _k in range(TOP_K):
        m = jnp.max(cur, -1, keepdims=True)
        ik = jnp.min(jnp.where(cur == m, lane_f, float(LANES)), -1, keepdims=True)
        hot = lane_f == ik
        vals.append(m)
        idxs.append(ik)
        hots.append(hot)
        cur = jnp.where(hot, -jnp.inf, cur)
    exps = [jnp.exp(vk - vals[0]) for vk in vals]
    denom = exps[0] + exps[1] + exps[2] + exps[3]
    count = jnp.zeros((tm, LANES), F32)
    for hot in hots:
        count = count + jnp.where(hot, 1.0, 0.0)
    strict = jnp.where(_iota((tm, tm), 1) < _iota((tm, tm), 0), 1.0, 0.0).astype(BF16)
    before = jnp.dot(strict, count.astype(BF16), preferred_element_type=F32) + carry_ref[...]
    out = jnp.zeros((tm, LANES), F32)
    for kk in range(TOP_K):
        rank = jnp.sum(jnp.where(hots[kk], before, 0.0), -1, keepdims=True)
        out = jnp.where(lane == ROUTE_IDX + kk, idxs[kk], out)
        out = jnp.where(lane == ROUTE_RANK + kk, rank, out)
        out = jnp.where(lane == ROUTE_GATE + kk, exps[kk] / denom, out)
    out_ref[...] = out
    total = carry_ref[...] + jnp.sum(count, 0, keepdims=True)
    carry_ref[...] = total
    cnt_ref[...] = total


def _route(x1, rw, rb):
    n = x1.shape[0]
    tm = TOK_TILE
    return pl.pallas_call(
        _route_kernel,
        grid=(n // tm,),
        in_specs=[pl.BlockSpec((tm, D_MODEL), lambda i: (i, 0)),
                  pl.BlockSpec((D_MODEL, LANES), lambda i: (0, 0)),
                  pl.BlockSpec((1, LANES), lambda i: (0, 0))],
        out_specs=[pl.BlockSpec((tm, LANES), lambda i: (i, 0)), pl.BlockSpec((1, LANES), lambda i: (0, 0))],
        out_shape=[jax.ShapeDtypeStruct((n, LANES), F32), jax.ShapeDtypeStruct((1, LANES), F32)],
        scratch_shapes=[pltpu.VMEM((1, LANES), F32)],
        compiler_params=_cparams(("arbitrary",)),
        name="router",
    )(x1, rw, rb)


def _row_copy(src, dst, sem):
    return pltpu.make_async_copy(src, dst, sem)


def _scatter_kernel(dest_ref, x_ref, xs_in_ref, xs_ref, sem):
    del xs_in_ref
    tm = x_ref.shape[0]

    def start(n, carry):
        for kk in range(TOP_K):
            d = dest_ref[n * TOP_K + kk]
            _row_copy(x_ref.at[pl.ds(n, 1), :], xs_ref.at[pl.ds(d, 1), :], sem).start(priority=kk % 2)
        return carry

    lax.fori_loop(0, tm, start, 0, unroll=DMA_UNROLL)

    def wait(n, carry):
        for kk in range(TOP_K):
            _row_copy(x_ref.at[pl.ds(0, 1), :], xs_ref.at[pl.ds(0, 1), :], sem).wait()
        return carry

    lax.fori_loop(0, tm, wait, 0, unroll=DMA_UNROLL)


def _scatter(dest, x1p, n_slots):
    n = x1p.shape[0]
    tm = TOK_TILE
    zeros = jnp.zeros((n_slots, HALF), U32)
    return pl.pallas_call(
        _scatter_kernel,
        grid=(n // tm,),
        in_specs=[pl.BlockSpec((tm * TOP_K,), lambda i: (i,), memory_space=pltpu.SMEM),
                  pl.BlockSpec((tm, HALF), lambda i: (i, 0)),
                  pl.BlockSpec(memory_space=pl.ANY)],
        out_specs=pl.BlockSpec(memory_space=pl.ANY),
        out_shape=jax.ShapeDtypeStruct((n_slots, HALF), U32),
        scratch_shapes=[pltpu.SemaphoreType.DMA(())],
        input_output_aliases={2: 0},
        compiler_params=_cparams(("arbitrary",)),
        name="moe_scatter",
    )(dest, x1p, zeros)


def _ffn_kernel(be_ref, nu_ref, xs_ref, wgu_ref, bgu_ref, wd_ref, bd_ref, ys_ref, wgu_b, wd_b):
    i = pl.program_id(0)
    live = i < nu_ref[0]
    fresh = jnp.logical_or(i == 0, be_ref[i] != be_ref[jnp.maximum(i - 1, 0)])

    @pl.when(jnp.logical_and(live, fresh))
    def _():
        wgu_b[...] = wgu_ref[0, 0].astype(BF16)
        wd_b[...] = wd_ref[0, 0].astype(BF16)

    @pl.when(live)
    def _():
        gu = jnp.dot(_unpack_rows(xs_ref[...]).astype(BF16), wgu_b[...], preferred_element_type=F32) + bgu_ref[0, 0]
        gate = jnp.minimum(gu[:, 0:D_EXPERT], SWIGLU_LIMIT)
        up = jnp.clip(gu[:, D_EXPERT:], -SWIGLU_LIMIT, SWIGLU_LIMIT)
        glu = gate * _sigmoid(SWIGLU_ALPHA * gate)
        act = ((up + 1.0) * glu).astype(BF16)
        ys_ref[...] = _pack_rows(jnp.dot(act, wd_b[...], preferred_element_type=F32) + bd_ref[0, 0])

    @pl.when(jnp.logical_not(live))
    def _():
        ys_ref[...] = jnp.zeros_like(ys_ref)


def _ffn(layer, block_e, n_used, xs, w_gu, b_gu, w_down, b_down):
    n_slots = xs.shape[0]
    bm = EXPERT_ROWS
    grid_spec = pltpu.PrefetchScalarGridSpec(
        num_scalar_prefetch=2,
        grid=(n_slots // bm,),
        in_specs=[pl.BlockSpec((bm, HALF), lambda i, be, nu: (i, 0)),
                  pl.BlockSpec((1, 1, D_MODEL, 2 * D_EXPERT), lambda i, be, nu: (layer, be[i], 0, 0)),
                  pl.BlockSpec((1, 1, 1, 2 * D_EXPERT), lambda i, be, nu: (layer, be[i], 0, 0)),
                  pl.BlockSpec((1, 1, D_EXPERT, D_MODEL), lambda i, be, nu: (layer, be[i], 0, 0)),
                  pl.BlockSpec((1, 1, 1, D_MODEL), lambda i, be, nu: (layer, be[i], 0, 0))],
        out_specs=pl.BlockSpec((bm, HALF), lambda i, be, nu: (i, 0)),
        scratch_shapes=[pltpu.VMEM((D_MODEL, 2 * D_EXPERT), BF16), pltpu.VMEM((D_EXPERT, D_MODEL), BF16)],
    )
    return pl.pallas_call(
        _ffn_kernel,
        grid_spec=grid_spec,
        out_shape=jax.ShapeDtypeStruct((n_slots, HALF), U32),
        compiler_params=pltpu.CompilerParams(dimension_semantics=("arbitrary",),
                                             vmem_limit_bytes=FFN_VMEM_LIMIT),
        name="moe_ffn",
    )(block_e, n_used, xs, w_gu, b_gu, w_down, b_down)


def _combine_kernel(dest_ref, base_ref, route_ref, ys_ref, g2_ref, b2_ref, xo_ref, xb_ref, buf, sem):
    tm = base_ref.shape[0]

    def start(n, carry):
        for kk in range(TOP_K):
            d = dest_ref[n * TOP_K + kk]
            _row_copy(ys_ref.at[pl.ds(d, 1), :], buf.at[kk, pl.ds(n, 1), :], sem).start(priority=kk % 2)
        return carry

    lax.fori_loop(0, tm, start, 0, unroll=DMA_UNROLL)

    def wait(n, carry):
        for kk in range(TOP_K):
            _row_copy(ys_ref.at[pl.ds(0, 1), :], buf.at[0, pl.ds(0, 1), :], sem).wait()
        return carry

    lax.fori_loop(0, tm, wait, 0, unroll=DMA_UNROLL)

    acc = base_ref[...]
    for kk in range(TOP_K):
        acc = acc + route_ref[:, ROUTE_GATE + kk:ROUTE_GATE + kk + 1] * _unpack_rows(buf[kk])
    out = _layernorm_rows(acc, g2_ref[...], b2_ref[...])
    xo_ref[...] = out
    xb_ref[...] = out.astype(BF16)


def _combine(dest, base, route, ys, g2, b2):
    n = base.shape[0]
    tm = TOK_TILE
    tok = lambda width: pl.BlockSpec((tm, width), lambda i: (i, 0))
    return pl.pallas_call(
        _combine_kernel,
        grid=(n // tm,),
        in_specs=[pl.BlockSpec((tm * TOP_K,), lambda i: (i,), memory_space=pltpu.SMEM),
                  tok(D_MODEL), tok(LANES), pl.BlockSpec(memory_space=pl.ANY),
                  pl.BlockSpec((1, D_MODEL), lambda i: (0, 0)), pl.BlockSpec((1, D_MODEL), lambda i: (0, 0))],
        out_specs=[tok(D_MODEL), tok(D_MODEL)],
        out_shape=[jax.ShapeDtypeStruct((n, D_MODEL), F32), jax.ShapeDtypeStruct((n, D_MODEL), BF16)],
        scratch_shapes=[pltpu.VMEM((TOP_K, tm, HALF), U32), pltpu.SemaphoreType.DMA(())],
        compiler_params=_cparams(("arbitrary",)),
        name="moe_combine_ln2",
    )(dest, base, route, ys, g2, b2)


def _row(v):
    return v.reshape(1, -1).astype(F32)


def _pad_rows(m, rows, offset):
    out = jnp.zeros((rows, m.shape[1]), m.dtype)
    return lax.dynamic_update_slice(out, m, (offset, 0))


def _regroup_w_in(w_in):
    d = w_in.shape[0]
    za = w_in[:, 0:A_MAIN]
    zb = w_in[:, 1792:3840]
    zc = w_in[:, 3840:5888]
    igfg = w_in[:, 5888:5896]
    zd = w_in[:, 5896:6920]
    zg = w_in[:, 6920:11016]
    vr = w_in[:, 11016:] if w_in.shape[1] > 11016 else jnp.zeros((d, 32), w_in.dtype)
    small = jnp.concatenate([vr, igfg, jnp.zeros((d, LANES - 40), w_in.dtype)], axis=1)
    a_grp = jnp.concatenate([za, small, jnp.zeros((d, 2048 - A_MAIN - LANES), w_in.dtype)], axis=1)
    return jnp.concatenate([zg, zb, zc, a_grp, zd], axis=1).astype(BF16)


def _block_diag(w):
    g, di, dj = w.shape
    eye = jnp.eye(g, dtype=w.dtype)
    return (eye[:, None, :, None] * w[:, :, None, :]).reshape(g * di, g * dj)


def kernel(x, p, w_in_first, w_in_rest, rwkv_mu, rwkv_w0, rwkv_w2, rwkv_a0, rwkv_a2, rwkv_v0, rwkv_v2,
           rwkv_g2, rwkv_k_k, rwkv_k_a, rwkv_r_k, rwkv_lnx_g, rwkv_lnx_b, hgrn_lower_bounds, hgrn_norm_w,
           mlstm_conv_w, mlstm_conv_b, mlstm_i_bias, mlstm_f_bias, mlstm_norm_w, lru_conv_w, lru_conv_b,
           lru_wx, lru_bx, lru_wa, lru_ba, lru_lambda, w_branch, w_out, ln1_g, ln1_b, router_w, router_b,
           expert_w_gu, expert_b_gu, expert_w_down, expert_b_down, ple_gate_w, ple_proj_w, ln2_g, ln2_b):
    bsz, seq, _ = x.shape
    n = bsz * seq
    n_assign = n * TOP_K
    n_blocks = n_assign // EXPERT_ROWS + N_EXPERTS
    n_slots = n_blocks * EXPERT_ROWS

    lb_all = jnp.cumsum(jax.nn.softmax(hgrn_lower_bounds.astype(F32), axis=0), axis=0)
    lb_all = lb_all - lb_all[0]

    xf = x.reshape(n, D_MODEL).astype(F32)
    xb = xf.astype(BF16)
    v_first = jnp.zeros((n, W), F32)
    for layer in range(DEPTH):
        first = layer == 0
        w_in = _regroup_w_in(w_in_first if first else w_in_rest[layer - 1])
        z = _proj(xb, w_in)

        rw_prm = {
            "mu": _row(rwkv_mu[layer]),
            "w0": _row(rwkv_w0[layer]),
            "w2": _pad_rows(rwkv_w2[layer], LANES, 0).astype(BF16),
            "a0": _row(rwkv_a0[layer]),
            "a2": _pad_rows(rwkv_a2[layer], LANES, 64).astype(BF16),
            "v0": _row(jnp.zeros((W,), F32) if first else rwkv_v0[layer - 1]),
            "v2": (jnp.zeros((LANES, W), BF16) if first
                   else _pad_rows(rwkv_v2[layer - 1], LANES, SMALL_VR).astype(BF16)),
            "g2": rwkv_g2[layer].astype(BF16),
            "k_k": _row(rwkv_k_k[layer]),
            "k_a": _row(rwkv_k_a[layer]),
            "r_k": _row(rwkv_r_k[layer]),
            "lnx_g": _row(rwkv_lnx_g[layer]),
            "lnx_b": _row(rwkv_lnx_b[layer]),
        }
        y_a, v_cur = _rwkv(z, v_first, rw_prm, bsz, seq, first)
        if first:
            v_first = v_cur
        y_b = _hgrn(z, _row(lb_all[layer]), _row(hgrn_norm_w[layer]), bsz, seq)
        gate_bias = jnp.zeros((LANES,), F32)
        gate_bias = gate_bias.at[SMALL_IG:SMALL_IG + MLSTM_HEADS].set(mlstm_i_bias[layer])
        gate_bias = gate_bias.at[SMALL_FG:SMALL_FG + MLSTM_HEADS].set(mlstm_f_bias[layer])
        y_c = _mlstm(z, mlstm_conv_w[layer].astype(F32), _row(mlstm_conv_b[layer]), _row(gate_bias),
                     _row(mlstm_norm_w[layer]), bsz, seq)
        y_d = _lru(z, lru_conv_w[layer].astype(F32), _row(lru_conv_b[layer]),
                   _block_diag(lru_wx[layer]).astype(BF16), _row(lru_bx[layer]),
                   _block_diag(lru_wa[layer]).astype(BF16), _row(lru_ba[layer]), _row(lru_lambda[layer]),
                   bsz, seq)

        x1, base, x1p = _merge(xf, (y_a, y_b, y_c, y_d), z, w_branch[layer].astype(BF16), w_out[layer].astype(BF16),
                          _row(ln1_g[layer]), _row(ln1_b[layer]), p[layer].reshape(n, D_PLE),
                          ple_gate_w[layer].astype(BF16), ple_proj_w[layer].astype(BF16))

        rw = jnp.zeros((D_MODEL, LANES), F32).at[:, 0:N_EXPERTS].set(router_w[layer])
        rb = jnp.zeros((1, LANES), F32).at[0, 0:N_EXPERTS].set(router_b[layer])
        route, counts_f = _route(x1, rw, rb)

        idx = route[:, ROUTE_IDX:ROUTE_IDX + TOP_K].astype(I32)
        rank = route[:, ROUTE_RANK:ROUTE_RANK + TOP_K].astype(I32)
        counts = counts_f[0, 0:N_EXPERTS].astype(I32)
        padded = (counts + EXPERT_ROWS - 1) // EXPERT_ROWS * EXPERT_ROWS
        pad_end = jnp.cumsum(padded)
        pad_start = pad_end - padded
        dest = (pad_start[idx] + rank).reshape(-1)
        block_start = jnp.arange(n_blocks, dtype=I32) * EXPERT_ROWS
        block_e = jnp.minimum(jnp.sum((pad_end[None, :] <= block_start[:, None]).astype(I32), axis=1),
                              N_EXPERTS - 1)
        n_used = (pad_end[-1:] // EXPERT_ROWS).astype(I32)

        xs = _scatter(dest, x1p, n_slots)
        ys = _ffn(layer, block_e, n_used, xs, expert_w_gu.astype(F32),
                  expert_b_gu.reshape(DEPTH, N_EXPERTS, 1, 2 * D_EXPERT).astype(F32),
                  expert_w_down.astype(F32),
                  expert_b_down.reshape(DEPTH, N_EXPERTS, 1, D_MODEL).astype(F32))
        xf, xb = _combine(dest, base, route, ys, _row(ln2_g[layer]), _row(ln2_b[layer]))
    return xf.reshape(bsz, seq, D_MODEL).astype(x.dtype)
```

```python
import functools
import math

import jax
import jax.numpy as jnp
from jax import lax
from jax.experimental import pallas as pl
from jax.experimental.pallas import tpu as pltpu

F32 = jnp.float32
BF16 = jnp.bfloat16
I32 = jnp.int32

D_MODEL = 1024
DEPTH = 4
W = 512
RWKV_HEAD_DIM = 64
RWKV_LNX_EPS = 64e-5
MLSTM_HEADS = 4
HEAD128 = 128
CHUNK = 64
LRU_C = 8.0
CONV_WIDTH = 4
N_EXPERTS = 32
TOP_K = 4
D_EXPERT = 1024
SWIGLU_LIMIT = 7.0
SWIGLU_ALPHA = 1.702
D_PLE = 256
LN_EPS = 1e-5
NORM_EPS = 1e-6
ALPHA = (2 * DEPTH) ** 0.25

COL_G = 0
COL_B = 4096
COL_C = 6144
COL_A = 8192
COL_D = 10240
N_COLS = 11264
A_MAIN = 1792
SMALL_VR = 0
SMALL_IG = 32
SMALL_FG = 36

EXPERT_ROWS = 512
TOK_TILE = 256
LANES = 128
DMA_UNROLL = 8
VMEM_LIMIT = 48 * 1024 * 1024
FFN_VMEM_LIMIT = 56 * 1024 * 1024


def _cparams(sem):
    return pltpu.CompilerParams(dimension_semantics=sem, vmem_limit_bytes=VMEM_LIMIT)


def _mm(a, b):
    return jnp.dot(a.astype(BF16), b.astype(BF16), preferred_element_type=F32)


def _mm_nt(a, b):
    return lax.dot_general(a.astype(BF16), b.astype(BF16), (((1,), (1,)), ((), ())),
                           preferred_element_type=F32)


def _mm_tn(a, b):
    return jnp.dot(a.T.astype(BF16), b.astype(BF16), preferred_element_type=F32)


def _split3(x):
    hi = x.astype(BF16)
    r1 = x - hi.astype(F32)
    mid = r1.astype(BF16)
    lo = (r1 - mid.astype(F32)).astype(BF16)
    return hi, mid, lo


def _mm_exact_lhs(m, x):
    hi, mid, lo = _split3(x)
    return (jnp.dot(m, hi, preferred_element_type=F32) + jnp.dot(m, mid, preferred_element_type=F32)
            + jnp.dot(m, lo, preferred_element_type=F32))


def _mm_exact_rhs(x, m):
    hi = x.astype(BF16)
    lo = (x - hi.astype(F32)).astype(BF16)
    return jnp.dot(hi, m, preferred_element_type=F32) + jnp.dot(lo, m, preferred_element_type=F32)


def _iota(shape, dim):
    return lax.broadcasted_iota(I32, shape, dim)


def _tri_incl(n):
    return jnp.where(_iota((n, n), 1) <= _iota((n, n), 0), 1.0, 0.0).astype(BF16)


def _seg_ones(n, seg):
    return jnp.where(_iota((n, n), 0) // seg == _iota((n, n), 1) // seg, 1.0, 0.0).astype(BF16)


def _sigmoid(x):
    return jax.nn.sigmoid(x)


def _silu(x):
    return x * jax.nn.sigmoid(x)


def _log_sigmoid(x):
    return jnp.minimum(x, 0.0) - jnp.log1p(jnp.exp(-jnp.abs(x)))


U32 = jnp.uint32
HALF = D_MODEL // 2


def _pack_rows(x):
    bits = lax.bitcast_convert_type(x.astype(BF16).astype(F32), U32)
    return (bits[:, 0:HALF] >> 16) | bits[:, HALF:]


def _unpack_rows(w):
    lo = lax.bitcast_convert_type(w << 16, F32)
    hi = lax.bitcast_convert_type(w & jnp.uint32(0xFFFF0000), F32)
    return jnp.concatenate([lo, hi], axis=1)


def _layernorm_rows(x, g, b):
    xc = x - jnp.mean(x, -1, keepdims=True)
    var = jnp.mean(xc * xc, -1, keepdims=True)
    return xc * lax.rsqrt(var + LN_EPS) * g + b


def _proj_kernel(x_ref, w_ref, o_ref):
    o_ref[...] = jnp.dot(x_ref[...], w_ref[...], preferred_element_type=F32).astype(o_ref.dtype)


def _proj(xb, w):
    n, k = xb.shape
    c = w.shape[1]
    tm = min(2048, n)
    tn = 1024
    return pl.pallas_call(
        _proj_kernel,
        grid=(n // tm, c // tn),
        in_specs=[pl.BlockSpec((tm, k), lambda i, j: (i, 0)),
                  pl.BlockSpec((k, tn), lambda i, j: (0, j))],
        out_specs=pl.BlockSpec((tm, tn), lambda i, j: (i, j)),
        out_shape=jax.ShapeDtypeStruct((n, c), BF16),
        compiler_params=_cparams(("parallel", "parallel")),
        name="in_proj",
    )(xb, w)


RWKV_CHUNKS = 4
RWKV_T = CHUNK * RWKV_CHUNKS


def _rwkv_kernel(first, z_ref, vf_ref, mu_ref, w0_ref, w2_ref, a0_ref, a2_ref, v0_ref, v2_ref, g2_ref,
                 kk_ref, ka_ref, rk_ref, lng_ref, lnb_ref, y_ref, vo_ref, prev_ref, s_ref):
    t_len = RWKV_T

    @pl.when(pl.program_id(1) == 0)
    def _():
        prev_ref[...] = jnp.zeros_like(prev_ref)
        s_ref[...] = jnp.zeros_like(s_ref)

    z = z_ref[:, 0:A_MAIN].astype(F32)
    small = z_ref[:, A_MAIN:A_MAIN + LANES].astype(F32)
    row = _iota((t_len, 1), 0)
    zs = jnp.where(row == 0, prev_ref[...], pltpu.roll(z, 1, 0))
    prev_ref[...] = z[t_len - 1:t_len, :]
    zz = z + (zs - z) * mu_ref[...]

    r = zz[:, 0:W]
    k = zz[:, W:2 * W]
    v = zz[:, 2 * W:3 * W]
    xwa = zz[:, 3 * W:3 * W + LANES]
    xg = zz[:, 3 * W + LANES:3 * W + 2 * LANES]

    u = w0_ref[...] + _mm(jnp.tanh(xwa), w2_ref[...])
    logw = -math.exp(-0.5) * _sigmoid(u)
    a_sig = _sigmoid(a0_ref[...] + _mm(xwa, a2_ref[...]))
    g = _mm(_sigmoid(xg), g2_ref[...])
    if not first:
        v = v + (vf_ref[...] - v) * _sigmoid(v0_ref[...] + _mm(small, v2_ref[...]))
    vo_ref[...] = v

    seg = _seg_ones(LANES, RWKV_HEAD_DIM)
    tri = jnp.where((_iota((t_len, t_len), 1) <= _iota((t_len, t_len), 0))
                    & (_iota((t_len, t_len), 1) // CHUNK == _iota((t_len, t_len), 0) // CHUNK),
                    1.0, 0.0).astype(BF16)
    c = _mm_exact_lhs(tri, logw)
    gam = jnp.exp(c)
    igam = jnp.exp(-c)
    gam_ex = jnp.exp(c - logw)

    lane = _iota((1, LANES), 1)
    lane2 = _iota((1, 2 * LANES), 1)
    hm = [lane < RWKV_HEAD_DIM, lane >= RWKV_HEAD_DIM]
    hm2 = [lane2 % LANES < RWKV_HEAD_DIM, lane2 % LANES >= RWKV_HEAD_DIM]
    bd_mask = _iota((LANES, LANES), 0) // RWKV_HEAD_DIM == _iota((LANES, LANES), 1) // RWKV_HEAD_DIM
    pstrict = (_iota((CHUNK, LANES), 1) % RWKV_HEAD_DIM) < _iota((CHUNK, LANES), 0)
    pincl = (_iota((CHUNK, LANES), 1) % RWKV_HEAD_DIM) <= _iota((CHUNK, LANES), 0)

    def bdiag(m, masks=hm):
        return jnp.concatenate([jnp.where(masks[0], m, 0.0), jnp.where(masks[1], m, 0.0)], axis=0)

    n_pair = W // LANES
    sls = [slice(p * LANES, (p + 1) * LANES) for p in range(n_pair)]
    rss = [slice(ci * CHUNK, (ci + 1) * CHUNK) for ci in range(RWKV_CHUNKS)]
    kk_raw = [k[:, sl] * kk_ref[:, sl] for sl in sls]
    kkn = [kr / jnp.maximum(jnp.sqrt(_mm_exact_rhs(kr * kr, seg)), 1e-12) for kr in kk_raw]
    aps = [a_sig[:, sl] for sl in sls]
    k2s = [k[:, sl] * (1.0 + (ap - 1.0) * ka_ref[:, sl]) for sl, ap in zip(sls, aps)]
    r_ts = [r[:, sl] * gam[:, sl] for sl in sls]
    a_ts = [-kn * gam_ex[:, sl] for kn, sl in zip(kkn, sls)]
    kb = [kn * ap for kn, ap in zip(kkn, aps)]
    b_ts = [x_ * igam[:, sl] for x_, sl in zip(kb, sls)]
    k_ts = [k2 * igam[:, sl] for k2, sl in zip(k2s, sls)]

    cps = [(ci, p) for ci in range(RWKV_CHUNKS) for p in range(n_pair)]
    lhs = [jnp.concatenate([a_ts[p][rss[ci]], r_ts[p][rss[ci]]], axis=0) for ci, p in cps]
    pb = [_mm_nt(lhs[i], bdiag(b_ts[p][rss[ci]])) for i, (ci, p) in enumerate(cps)]
    pk = [_mm_nt(lhs[i], bdiag(k_ts[p][rss[ci]])) for i, (ci, p) in enumerate(cps)]
    v_bd = [bdiag(v[rss[ci], sls[p]]) for ci, p in cps]
    a_ak = [jnp.where(pstrict, m[0:CHUNK], 0.0) for m in pk]
    a_rb = [jnp.where(pincl, m[CHUNK:], 0.0) for m in pb]
    a_rk = [jnp.where(pincl, m[CHUNK:], 0.0) for m in pk]
    pw = [jnp.where(pstrict, m[0:CHUNK], 0.0) for m in pb]
    w2 = [jnp.concatenate([_mm(a_ak[i], v_bd[i]), a_ts[p][rss[ci]]], axis=1) for i, (ci, p) in enumerate(cps)]
    for it in range(6):
        w2 = [x_ + _mm(q_, bdiag(x_, hm2)) for x_, q_ in zip(w2, pw)]
        if it < 5:
            pw = [_mm(q_, bdiag(q_)) for q_ in pw]

    s_cur = [s_ref[p] for p in range(n_pair)]
    y_rows = [[] for _ in range(n_pair)]
    for ci in range(RWKV_CHUNKS):
        rs = rss[ci]
        idx = [ci * n_pair + p for p in range(n_pair)]
        c_last = c[ci * CHUNK + CHUNK - 1:ci * CHUNK + CHUNK, :]
        tail = jnp.exp(c_last - c[rs])
        g_last = jnp.exp(c_last)
        ls0 = [_mm_nt(jnp.concatenate([w2[idx[p]][:, LANES:], r_ts[p][rs]], axis=0), s_cur[p])
               for p in range(n_pair)]
        xs = [w2[idx[p]][:, 0:LANES] + ls0[p][0:CHUNK] for p in range(n_pair)]
        ys = [ls0[p][CHUNK:] + _mm(jnp.concatenate([a_rb[idx[p]], a_rk[idx[p]]], axis=1),
                                   jnp.concatenate([bdiag(xs[p]), v_bd[idx[p]]], axis=0))
              for p in range(n_pair)]
        upd = [_mm_tn(jnp.concatenate([xs[p], v[rs, sls[p]]], axis=0),
                      jnp.concatenate([kb[p][rs] * tail[:, sls[p]], k2s[p][rs] * tail[:, sls[p]]], axis=0))
               for p in range(n_pair)]
        s_cur = [s_cur[p] * g_last[:, sls[p]] + jnp.where(bd_mask, upd[p], 0.0) for p in range(n_pair)]
        for p in range(n_pair):
            y_rows[p].append(ys[p])
    for p in range(n_pair):
        s_ref[p] = s_cur[p]

    y_pairs = [jnp.concatenate(rows_, axis=0) for rows_ in y_rows]
    means = [_mm_exact_rhs(y_, seg) * (1.0 / RWKV_HEAD_DIM) for y_ in y_pairs]
    ycs = [y_ - m_ for y_, m_ in zip(y_pairs, means)]
    vars_ = [_mm_exact_rhs(yc * yc, seg) * (1.0 / RWKV_HEAD_DIM) for yc in ycs]
    bonus = [_mm_exact_rhs(r[:, sls[p]] * k2s[p] * rk_ref[:, sls[p]], seg) * v[:, sls[p]] for p in range(n_pair)]
    for p in range(n_pair):
        sl = sls[p]
        yn = ycs[p] * lax.rsqrt(vars_[p] + RWKV_LNX_EPS) * lng_ref[:, sl] + lnb_ref[:, sl]
        y_ref[:, sl] = ((yn + bonus[p]) * g[:, sl]).astype(y_ref.dtype)


def _rwkv(z, v_first, prm, bsz, seq, first):
    n = bsz * seq
    nj = seq // RWKV_T
    row_spec = lambda width: pl.BlockSpec((1, width), lambda b, j: (0, 0))
    mat_spec = lambda r_, c_: pl.BlockSpec((r_, c_), lambda b, j: (0, 0))
    tok = lambda width: pl.BlockSpec((RWKV_T, width), lambda b, j: (b * nj + j, 0))
    in_specs = [
        pl.BlockSpec((RWKV_T, 2048), lambda b, j: (b * nj + j, COL_A // 2048)),
        tok(W),
        row_spec(A_MAIN), row_spec(W), mat_spec(LANES, W), row_spec(W), mat_spec(LANES, W),
        row_spec(W), mat_spec(LANES, W), mat_spec(LANES, W),
        row_spec(W), row_spec(W), row_spec(W), row_spec(W), row_spec(W),
    ]
    return pl.pallas_call(
        functools.partial(_rwkv_kernel, first),
        grid=(bsz, nj),
        in_specs=in_specs,
        out_specs=[tok(W), tok(W)],
        out_shape=[jax.ShapeDtypeStruct((n, W), BF16), jax.ShapeDtypeStruct((n, W), F32)],
        scratch_shapes=[pltpu.VMEM((1, A_MAIN), F32), pltpu.VMEM((W // LANES, LANES, LANES), F32)],
        compiler_params=_cparams(("parallel", "arbitrary")),
        name="rwkv7",
    )(z, v_first, prm["mu"], prm["w0"], prm["w2"], prm["a0"], prm["a2"], prm["v0"], prm["v2"], prm["g2"],
      prm["k_k"], prm["k_a"], prm["r_k"], prm["lnx_g"], prm["lnx_b"])


HG_SUB = 16


def _hgrn_kernel(z_ref, lb_ref, nw_ref, y_ref, s_ref, q_s, k_s, v_s, bc_s, o_s):
    t_len = CHUNK

    @pl.when(pl.program_id(1) == 0)
    def _():
        s_ref[...] = jnp.zeros_like(s_ref)

    zq = z_ref[:, 0:W].astype(F32)
    f = z_ref[:, W:2 * W].astype(F32)
    lb = lb_ref[...]
    q_s[...] = _silu(zq)
    v_s[...] = z_ref[:, 2 * W:3 * W].astype(F32)
    la = jnp.log(lb)
    lbb = jnp.log1p(-lb) + _log_sigmoid(f)
    logf = jnp.maximum(la, lbb) + jnp.log1p(jnp.exp(-jnp.abs(la - lbb)))
    k_s[...] = (1.0 - lb) * _sigmoid(-f)
    blk = jnp.where((_iota((t_len, t_len), 1) <= _iota((t_len, t_len), 0))
                    & (_iota((t_len, t_len), 1) // HG_SUB == _iota((t_len, t_len), 0) // HG_SUB),
                    1.0, 0.0).astype(BF16)
    bc_s[...] = _mm_exact_lhs(blk, logf)

    row8 = _iota((HG_SUB // 2, 1), 0)

    def sub_block(i):
        r0 = i * HG_SUB
        n_head = W // HEAD128
        sls = [slice(h * HEAD128, (h + 1) * HEAD128) for h in range(n_head)]
        qb = q_s[pl.ds(r0, HG_SUB), :]
        kb = k_s[pl.ds(r0, HG_SUB), :]
        vb = v_s[pl.ds(r0, HG_SUB), :]
        bcb = bc_s[pl.ds(r0, HG_SUB), :]
        st = [s_ref[h] for h in range(n_head)]
        qd = qb * jnp.exp(bcb)
        o = [_mm_nt(qd[:, sl], st[h]) for h, sl in enumerate(sls)]
        half = HG_SUB // 2
        o_top = [x_[0:half] for x_ in o]
        o_bot = [x_[half:] for x_ in o]
        for s in range(HG_SUB):
            bs = bcb[s:s + 1, :]
            ks = kb[s:s + 1, :]
            if s < half:
                p_top = qb[0:half] * ks * jnp.exp(jnp.where(row8 >= s, bcb[0:half] - bs, -jnp.inf))
                p_bot = qb[half:] * ks * jnp.exp(bcb[half:] - bs)
            else:
                p_top = None
                p_bot = qb[half:] * ks * jnp.exp(jnp.where(row8 >= s - half, bcb[half:] - bs, -jnp.inf))
            for h, sl in enumerate(sls):
                vs = vb[s:s + 1, sl]
                if p_top is not None:
                    o_top[h] = o_top[h] + jnp.sum(p_top[:, sl], axis=-1, keepdims=True) * vs
                o_bot[h] = o_bot[h] + jnp.sum(p_bot[:, sl], axis=-1, keepdims=True) * vs
        bl = bcb[HG_SUB - 1:HG_SUB, :]
        kd = kb * jnp.exp(bl - bcb)
        e_last = jnp.exp(bl)
        upd = [_mm_tn(vb[:, sl], kd[:, sl]) for sl in sls]
        for h, sl in enumerate(sls):
            s_ref[h] = st[h] * e_last[:, sl] + upd[h]
            o_s[pl.ds(r0, half), sl] = o_top[h]
            o_s[pl.ds(r0 + half, half), sl] = o_bot[h]

    for i in range(t_len // HG_SUB):
        sub_block(i)

    g = z_ref[:, 3 * W:4 * W].astype(F32)
    for h in range(W // HEAD128):
        sl = slice(h * HEAD128, (h + 1) * HEAD128)
        o = o_s[:, sl]
        o = o * lax.rsqrt(jnp.mean(o * o, -1, keepdims=True) + NORM_EPS) * nw_ref[:, sl]
        y_ref[:, sl] = (o * _silu(g[:, sl])).astype(y_ref.dtype)


def _hgrn(z, lb, norm_w, bsz, seq):
    n = bsz * seq
    nj = seq // CHUNK
    row_spec = pl.BlockSpec((1, W), lambda b, j: (0, 0))
    return pl.pallas_call(
        _hgrn_kernel,
        grid=(bsz, nj),
        in_specs=[pl.BlockSpec((CHUNK, 2048), lambda b, j: (b * nj + j, COL_B // 2048)), row_spec, row_spec],
        out_specs=pl.BlockSpec((CHUNK, W), lambda b, j: (b * nj + j, 0)),
        out_shape=jax.ShapeDtypeStruct((n, W), BF16),
        scratch_shapes=[pltpu.VMEM((W // HEAD128, HEAD128, HEAD128), F32)]
        + [pltpu.VMEM((CHUNK, W), F32) for _ in range(5)],
        compiler_params=_cparams(("parallel", "arbitrary")),
        name="hgrn2",
    )(z, lb, norm_w)


def _mlstm_kernel(z_ref, sm_ref, cw_ref, cb_ref, gb_ref, nw_ref, y_ref, ext_ref, c_ref, n_ref, m_ref):
    t_len = CHUNK
    pad = 8

    @pl.when(pl.program_id(1) == 0)
    def _():
        ext_ref[0:pad, :] = jnp.zeros((pad, 2 * W), F32)
        c_ref[...] = jnp.zeros_like(c_ref)
        n_ref[...] = jnp.zeros_like(n_ref)
        m_ref[...] = jnp.full(m_ref.shape, -1e30, F32)

    ext_ref[pad:pad + t_len, :] = z_ref[:, 0:2 * W].astype(F32)
    acc = cb_ref[...]
    for jj in range(CONV_WIDTH):
        acc = acc + ext_ref[pl.ds(pad - (CONV_WIDTH - 1) + jj, t_len), :] * cw_ref[jj:jj + 1, :]
    ext_ref[0:pad, :] = ext_ref[t_len:t_len + pad, :]
    qk = _silu(acc)
    q = qk[:, 0:W]
    k = qk[:, W:2 * W] * (HEAD128 ** -0.5)
    v = z_ref[:, 2 * W:3 * W].astype(F32)
    og = z_ref[:, 3 * W:4 * W].astype(F32)

    gates = sm_ref[...].astype(F32) + gb_ref[...]
    lf = _log_sigmoid(gates)
    fc = _mm_exact_lhs(_tri_incl(t_len), lf)
    gates_t = gates.T
    fc_t = fc.T
    causal = _iota((t_len, t_len), 1) <= _iota((t_len, t_len), 0)

    hs = range(MLSTM_HEADS)
    sls = [slice(h * HEAD128, (h + 1) * HEAD128) for h in hs]
    ic_col = [gates[:, SMALL_IG + h:SMALL_IG + h + 1] for h in hs]
    ic_row = [gates_t[SMALL_IG + h:SMALL_IG + h + 1, :] for h in hs]
    fc_col = [fc[:, SMALL_FG + h:SMALL_FG + h + 1] for h in hs]
    fc_row = [fc_t[SMALL_FG + h:SMALL_FG + h + 1, :] for h in hs]
    m_prev = [m_ref[h:h + 1, 0:1] for h in hs]
    c_mat = [c_ref[h] for h in hs]
    n_vec = [n_ref[h:h + 1, :] for h in hs]
    qk_raw = [_mm_nt(q[:, sl], k[:, sl]) for sl in sls]
    q_c = [_mm(q[:, sl], c_mat[h]) for h, sl in enumerate(sls)]
    log_w = [jnp.where(causal, fc_col[h] - fc_row[h] + ic_row[h], -jnp.inf) for h in hs]
    log_inter = [fc_col[h] + m_prev[h] for h in hs]
    m_t = [jnp.maximum(log_inter[h], jnp.max(log_w[h], -1, keepdims=True)) for h in hs]
    w_inter = [jnp.exp(log_inter[h] - m_t[h]) for h in hs]
    s_qk = [qk_raw[h] * jnp.exp(log_w[h] - m_t[h]) for h in hs]
    s_v = [_mm(s_qk[h], v[:, sl]) for h, sl in enumerate(sls)]
    f_last = [fc_col[h][t_len - 1:t_len, :] for h in hs]
    log_s = [f_last[h] - fc_col[h] + ic_col[h] for h in hs]
    m_new = [jnp.maximum(f_last[h] + m_prev[h], jnp.max(log_s[h], 0, keepdims=True)) for h in hs]
    kw = [k[:, sl] * jnp.exp(log_s[h] - m_new[h]) for h, sl in enumerate(sls)]
    kv = [_mm_tn(kw[h], v[:, sl]) for h, sl in enumerate(sls)]
    for h, sl in enumerate(sls):
        num = w_inter[h] * q_c[h] + s_v[h]
        den = (w_inter[h] * jnp.sum(q[:, sl] * n_vec[h], -1, keepdims=True)
               + jnp.sum(s_qk[h], -1, keepdims=True))
        hid = num / jnp.maximum(jnp.abs(den), jnp.exp(-m_t[h]))
        decay = jnp.exp(f_last[h] + m_prev[h] - m_new[h])
        c_ref[h] = decay * c_mat[h] + kv[h]
        n_ref[h:h + 1, :] = decay * n_vec[h] + jnp.sum(kw[h], 0, keepdims=True)
        m_ref[h:h + 1, :] = jnp.broadcast_to(m_new[h], (1, LANES))
        hc = hid - jnp.mean(hid, -1, keepdims=True)
        hn = hc * lax.rsqrt(jnp.mean(hc * hc, -1, keepdims=True) + NORM_EPS) * nw_ref[:, sl]
        y_ref[:, sl] = (_sigmoid(og[:, sl]) * hn).astype(y_ref.dtype)


def _mlstm(z, conv_w, conv_b, gate_bias, norm_w, bsz, seq):
    n = bsz * seq
    nj = seq // CHUNK
    const = lambda r_, c_: pl.BlockSpec((r_, c_), lambda b, j: (0, 0))
    return pl.pallas_call(
        _mlstm_kernel,
        grid=(bsz, nj),
        in_specs=[pl.BlockSpec((CHUNK, 2048), lambda b, j: (b * nj + j, COL_C // 2048)),
                  pl.BlockSpec((CHUNK, LANES), lambda b, j: (b * nj + j, (COL_A + A_MAIN) // LANES)),
                  const(CONV_WIDTH, 2 * W), const(1, 2 * W), const(1, LANES), const(1, W)],
        out_specs=pl.BlockSpec((CHUNK, W), lambda b, j: (b * nj + j, 0)),
        out_shape=jax.ShapeDtypeStruct((n, W), BF16),
        scratch_shapes=[pltpu.VMEM((CHUNK + 8, 2 * W), F32),
                        pltpu.VMEM((MLSTM_HEADS, HEAD128, HEAD128), F32),
                        pltpu.VMEM((8, HEAD128), F32), pltpu.VMEM((8, LANES), F32)],
        compiler_params=_cparams(("parallel", "arbitrary")),
        name="mlstm",
    )(z, z, conv_w, conv_b, gate_bias, norm_w)


LRU_T = 256


def _lru_kernel(z_ref, cw_ref, cb_ref, wx_ref, bx_ref, wa_ref, ba_ref, lam_ref, y_ref, ext_ref, h_ref):
    t_len = LRU_T
    pad = 8
    first_blk = pl.program_id(1) == 0

    @pl.when(first_blk)
    def _():
        ext_ref[0:pad, :] = jnp.zeros((pad, W), F32)
        h_ref[...] = jnp.zeros_like(h_ref)

    ext_ref[pad:pad + t_len, :] = z_ref[:, 0:W].astype(F32)
    xc = cb_ref[...]
    for jj in range(CONV_WIDTH):
        xc = xc + ext_ref[pl.ds(pad - (CONV_WIDTH - 1) + jj, t_len), :] * cw_ref[jj:jj + 1, :]
    ext_ref[0:pad, :] = ext_ref[t_len:t_len + pad, :]

    gate_x = _sigmoid(_mm(xc, wx_ref[...]) + bx_ref[...])
    gate_a = _sigmoid(_mm(xc, wa_ref[...]) + ba_ref[...])
    log_a = LRU_C * gate_a * _log_sigmoid(lam_ref[...])
    a = jnp.exp(log_a)
    mult = jnp.sqrt(-jnp.tanh(log_a) * (a * a + 1.0))
    row = _iota((t_len, 1), 0)
    mult = jnp.where(jnp.logical_and(first_blk, row == 0), 1.0, mult)
    b = mult * gate_x * xc

    d = 1
    while d < t_len:
        keep = row >= d
        a_sh = jnp.where(keep, pltpu.roll(a, d, 0), 1.0)
        b_sh = jnp.where(keep, pltpu.roll(b, d, 0), 0.0)
        b = a * b_sh + b
        a = a * a_sh
        d *= 2
    h = a * h_ref[...] + b
    h_ref[...] = h[t_len - 1:t_len, :]
    y_ref[...] = (h * jax.nn.gelu(z_ref[:, W:2 * W].astype(F32), approximate=True)).astype(y_ref.dtype)


def _lru(z, conv_w, conv_b, wx, bx, wa, ba, lam, bsz, seq):
    n = bsz * seq
    nj = seq // LRU_T
    const = lambda r_, c_: pl.BlockSpec((r_, c_), lambda b, j: (0, 0))
    return pl.pallas_call(
        _lru_kernel,
        grid=(bsz, nj),
        in_specs=[pl.BlockSpec((LRU_T, 2 * W), lambda b, j: (b * nj + j, COL_D // (2 * W))),
                  const(CONV_WIDTH, W), const(1, W), const(W, W), const(1, W), const(W, W), const(1, W),
                  const(1, W)],
        out_specs=pl.BlockSpec((LRU_T, W), lambda b, j: (b * nj + j, 0)),
        out_shape=jax.ShapeDtypeStruct((n, W), BF16),
        scratch_shapes=[pltpu.VMEM((LRU_T + 8, W), F32), pltpu.VMEM((1, W), F32)],
        compiler_params=_cparams(("parallel", "arbitrary")),
        name="rglru",
    )(z, conv_w, conv_b, wx, bx, wa, ba, lam)


def _merge_kernel(x_ref, ya_ref, yb_ref, yc_ref, yd_ref, zg_ref, wb_ref, wo_ref, g1_ref, b1_ref,
                  p_ref, wpg_ref, wpp_ref, x1_ref, base_ref, x1p_ref):
    merged = None
    for nb, y_ref in enumerate((ya_ref, yb_ref, yc_ref, yd_ref)):
        proj = jnp.dot(y_ref[...], wb_ref[nb], preferred_element_type=F32)
        term = _sigmoid(zg_ref[:, nb * D_MODEL:(nb + 1) * D_MODEL].astype(F32)) * proj
        merged = term if merged is None else merged + term
    mix = _mm(merged, wo_ref[...])
    x1 = _layernorm_rows(ALPHA * x_ref[...] + mix, g1_ref[...], b1_ref[...])
    x1_ref[...] = x1
    x1p_ref[...] = _pack_rows(x1)
    ple = _sigmoid(_mm(x1, wpg_ref[...])) * _mm(p_ref[...], wpp_ref[...])
    base_ref[...] = ALPHA * x1 + ple


def _merge(x, ys, z, w_branch, w_out, g1, b1, p_l, w_pg, w_pp):
    n = x.shape[0]
    tm = TOK_TILE
    tok = lambda width: pl.BlockSpec((tm, width), lambda i: (i, 0))
    const2 = lambda r_, c_: pl.BlockSpec((r_, c_), lambda i: (0, 0))
    return pl.pallas_call(
        _merge_kernel,
        grid=(n // tm,),
        in_specs=[tok(D_MODEL), tok(W), tok(W), tok(W), tok(W),
                  pl.BlockSpec((tm, 4 * D_MODEL), lambda i: (i, COL_G // (4 * D_MODEL))),
                  pl.BlockSpec((4, W, D_MODEL), lambda i: (0, 0, 0)), const2(D_MODEL, D_MODEL),
                  const2(1, D_MODEL), const2(1, D_MODEL), tok(D_PLE), const2(D_MODEL, D_MODEL),
                  const2(D_PLE, D_MODEL)],
        out_specs=[tok(D_MODEL), tok(D_MODEL), tok(HALF)],
        out_shape=[jax.ShapeDtypeStruct((n, D_MODEL), F32), jax.ShapeDtypeStruct((n, D_MODEL), F32),
                   jax.ShapeDtypeStruct((n, HALF), U32)],
        compiler_params=_cparams(("parallel",)),
        name="merge_ln1",
    )(x, *ys, z, w_branch, w_out, g1, b1, p_l, w_pg, w_pp)


ROUTE_IDX = 0
ROUTE_RANK = 4
ROUTE_GATE = 8


def _route_kernel(x_ref, rw_ref, rb_ref, out_ref, cnt_ref, carry_ref):
    tm = x_ref.shape[0]

    @pl.when(pl.program_id(0) == 0)
    def _():
        carry_ref[...] = jnp.zeros_like(carry_ref)

    xh, xm, _ = _split3(x_ref[...])
    wh, wm, _ = _split3(rw_ref[...])
    logits = (jnp.dot(xh, wh, preferred_element_type=F32) + jnp.dot(xh, wm, preferred_element_type=F32)
              + jnp.dot(xm, wh, preferred_element_type=F32)) + rb_ref[...]
    lane = _iota((tm, LANES), 1)
    lane_f = lane.astype(F32)
    cur = jnp.where(lane < N_EXPERTS, logits, -jnp.inf)
    vals, idxs, hots = [], [], []
    for _k in range(TOP_K):
        m = jnp.max(cur, -1, keepdims=True)
        ik = jnp.min(jnp.where(cur == m, lane_f, float(LANES)), -1, keepdims=True)
        hot = lane_f == ik
        vals.append(m)
        idxs.append(ik)
        hots.append(hot)
        cur = jnp.where(hot, -jnp.inf, cur)
    exps = [jnp.exp(vk - vals[0]) for vk in vals]
    denom = exps[0] + exps[1] + exps[2] + exps[3]
    count = jnp.zeros((tm, LANES), F32)
    for hot in hots:
        count = count + jnp.where(hot, 1.0, 0.0)
    strict = jnp.where(_iota((tm, tm), 1) < _iota((tm, tm), 0), 1.0, 0.0).astype(BF16)
    before = jnp.dot(strict, count.astype(BF16), preferred_element_type=F32) + carry_ref[...]
    out = jnp.zeros((tm, LANES), F32)
    for kk in range(TOP_K):
        rank = jnp.sum(jnp.where(hots[kk], before, 0.0), -1, keepdims=True)
        out = jnp.where(lane == ROUTE_IDX + kk, idxs[kk], out)
        out = jnp.where(lane == ROUTE_RANK + kk, rank, out)
        out = jnp.where(lane == ROUTE_GATE + kk, exps[kk] / denom, out)
    out_ref[...] = out
    total = carry_ref[...] + jnp.sum(count, 0, keepdims=True)
    carry_ref[...] = total
    cnt_ref[...] = total


def _route(x1, rw, rb):
    n = x1.shape[0]
    tm = TOK_TILE
    return pl.pallas_call(
        _route_kernel,
        grid=(n // tm,),
        in_specs=[pl.BlockSpec((tm, D_MODEL), lambda i: (i, 0)),
                  pl.BlockSpec((D_MODEL, LANES), lambda i: (0, 0)),
                  pl.BlockSpec((1, LANES), lambda i: (0, 0))],
        out_specs=[pl.BlockSpec((tm, LANES), lambda i: (i, 0)), pl.BlockSpec((1, LANES), lambda i: (0, 0))],
        out_shape=[jax.ShapeDtypeStruct((n, LANES), F32), jax.ShapeDtypeStruct((1, LANES), F32)],
        scratch_shapes=[pltpu.VMEM((1, LANES), F32)],
        compiler_params=_cparams(("arbitrary",)),
        name="router",
    )(x1, rw, rb)


def _row_copy(src, dst, sem):
    return pltpu.make_async_copy(src, dst, sem)


def _scatter_kernel(dest_ref, x_ref, xs_in_ref, xs_ref, sem):
    del xs_in_ref
    tm = x_ref.shape[0]

    def start(n, carry):
        for kk in range(TOP_K):
            d = dest_ref[n * TOP_K + kk]
            _row_copy(x_ref.at[pl.ds(n, 1), :], xs_ref.at[pl.ds(d, 1), :], sem).start(priority=kk % 2)
        return carry

    lax.fori_loop(0, tm, start, 0, unroll=DMA_UNROLL)

    def wait(n, carry):
        for kk in range(TOP_K):
            _row_copy(x_ref.at[pl.ds(0, 1), :], xs_ref.at[pl.ds(0, 1), :], sem).wait()
        return carry

    lax.fori_loop(0, tm, wait, 0, unroll=DMA_UNROLL)


def _scatter(dest, x1p, n_slots):
    n = x1p.shape[0]
    tm = TOK_TILE
    zeros = jnp.zeros((n_slots, HALF), U32)
    return pl.pallas_call(
        _scatter_kernel,
        grid=(n // tm,),
        in_specs=[pl.BlockSpec((tm * TOP_K,), lambda i: (i,), memory_space=pltpu.SMEM),
                  pl.BlockSpec((tm, HALF), lambda i: (i, 0)),
                  pl.BlockSpec(memory_space=pl.ANY)],
        out_specs=pl.BlockSpec(memory_space=pl.ANY),
        out_shape=jax.ShapeDtypeStruct((n_slots, HALF), U32),
        scratch_shapes=[pltpu.SemaphoreType.DMA(())],
        input_output_aliases={2: 0},
        compiler_params=_cparams(("arbitrary",)),
        name="moe_scatter",
    )(dest, x1p, zeros)


def _ffn_kernel(be_ref, nu_ref, xs_ref, wgu_ref, bgu_ref, wd_ref, bd_ref, ys_ref, wgu_b, wd_b):
    i = pl.program_id(0)
    live = i < nu_ref[0]
    fresh = jnp.logical_or(i == 0, be_ref[i] != be_ref[jnp.maximum(i - 1, 0)])

    @pl.when(jnp.logical_and(live, fresh))
    def _():
        wgu_b[...] = wgu_ref[0, 0].astype(BF16)
        wd_b[...] = wd_ref[0, 0].astype(BF16)

    @pl.when(live)
    def _():
        gu = jnp.dot(_unpack_rows(xs_ref[...]).astype(BF16), wgu_b[...], preferred_element_type=F32) + bgu_ref[0, 0]
        gate = jnp.minimum(gu[:, 0:D_EXPERT], SWIGLU_LIMIT)
        up = jnp.clip(gu[:, D_EXPERT:], -SWIGLU_LIMIT, SWIGLU_LIMIT)
        glu = gate * _sigmoid(SWIGLU_ALPHA * gate)
        act = ((up + 1.0) * glu).astype(BF16)
        ys_ref[...] = _pack_rows(jnp.dot(act, wd_b[...], preferred_element_type=F32) + bd_ref[0, 0])

    @pl.when(jnp.logical_not(live))
    def _():
        ys_ref[...] = jnp.zeros_like(ys_ref)


def _ffn(layer, block_e, n_used, xs, w_gu, b_gu, w_down, b_down):
    n_slots = xs.shape[0]
    bm = EXPERT_ROWS
    grid_spec = pltpu.PrefetchScalarGridSpec(
        num_scalar_prefetch=2,
        grid=(n_slots // bm,),
        in_specs=[pl.BlockSpec((bm, HALF), lambda i, be, nu: (i, 0)),
                  pl.BlockSpec((1, 1, D_MODEL, 2 * D_EXPERT), lambda i, be, nu: (layer, be[i], 0, 0)),
                  pl.BlockSpec((1, 1, 1, 2 * D_EXPERT), lambda i, be, nu: (layer, be[i], 0, 0)),
                  pl.BlockSpec((1, 1, D_EXPERT, D_MODEL), lambda i, be, nu: (layer, be[i], 0, 0)),
                  pl.BlockSpec((1, 1, 1, D_MODEL), lambda i, be, nu: (layer, be[i], 0, 0))],
        out_specs=pl.BlockSpec((bm, HALF), lambda i, be, nu: (i, 0)),
        scratch_shapes=[pltpu.VMEM((D_MODEL, 2 * D_EXPERT), BF16), pltpu.VMEM((D_EXPERT, D_MODEL), BF16)],
    )
    return pl.pallas_call(
        _ffn_kernel,
        grid_spec=grid_spec,
        out_shape=jax.ShapeDtypeStruct((n_slots, HALF), U32),
        compiler_params=pltpu.CompilerParams(dimension_semantics=("arbitrary",),
                                             vmem_limit_bytes=FFN_VMEM_LIMIT),
        name="moe_ffn",
    )(block_e, n_used, xs, w_gu, b_gu, w_down, b_down)


def _combine_kernel(dest_ref, dnext_ref, base_ref, route_ref, ys_ref, g2_ref, b2_ref, xo_ref, xb_ref, buf, sems):
    tm = base_ref.shape[0]
    step = pl.program_id(0)
    slot = step % 2

    def issue(idx_ref, s):
        def start(n, carry):
            for kk in range(TOP_K):
                d = idx_ref[n * TOP_K + kk]
                _row_copy(ys_ref.at[pl.ds(d, 1), :], buf.at[s, kk, pl.ds(n, 1), :],
                          sems.at[s]).start(priority=kk % 2)
            return carry

        lax.fori_loop(0, tm, start, 0, unroll=DMA_UNROLL)

    @pl.when(step == 0)
    def _():
        issue(dest_ref, 0)

    @pl.when(step + 1 < pl.num_programs(0))
    def _():
        issue(dnext_ref, 1 - slot)

    def wait(n, carry):
        for kk in range(TOP_K):
            _row_copy(ys_ref.at[pl.ds(0, 1), :], buf.at[slot, 0, pl.ds(0, 1), :], sems.at[slot]).wait()
        return carry

    lax.fori_loop(0, tm, wait, 0, unroll=DMA_UNROLL)

    acc = base_ref[...]
    for kk in range(TOP_K):
        acc = acc + route_ref[:, ROUTE_GATE + kk:ROUTE_GATE + kk + 1] * _unpack_rows(buf[slot, kk])
    out = _layernorm_rows(acc, g2_ref[...], b2_ref[...])
    xo_ref[...] = out
    xb_ref[...] = out.astype(BF16)


def _combine(dest, base, route, ys, g2, b2):
    n = base.shape[0]
    tm = TOK_TILE
    tok = lambda width: pl.BlockSpec((tm, width), lambda i: (i, 0))
    n_steps = n // tm
    return pl.pallas_call(
        _combine_kernel,
        grid=(n_steps,),
        in_specs=[pl.BlockSpec((tm * TOP_K,), lambda i: (i,), memory_space=pltpu.SMEM),
                  pl.BlockSpec((tm * TOP_K,), lambda i: (jnp.minimum(i + 1, n_steps - 1),),
                               memory_space=pltpu.SMEM),
                  tok(D_MODEL), tok(LANES), pl.BlockSpec(memory_space=pl.ANY),
                  pl.BlockSpec((1, D_MODEL), lambda i: (0, 0)), pl.BlockSpec((1, D_MODEL), lambda i: (0, 0))],
        out_specs=[tok(D_MODEL), tok(D_MODEL)],
        out_shape=[jax.ShapeDtypeStruct((n, D_MODEL), F32), jax.ShapeDtypeStruct((n, D_MODEL), BF16)],
        scratch_shapes=[pltpu.VMEM((2, TOP_K, tm, HALF), U32), pltpu.SemaphoreType.DMA((2,))],
        compiler_params=_cparams(("arbitrary",)),
        name="moe_combine_ln2",
    )(dest, dest, base, route, ys, g2, b2)


def _row(v):
    return v.reshape(1, -1).astype(F32)


def _pad_rows(m, rows, offset):
    out = jnp.zeros((rows, m.shape[1]), m.dtype)
    return lax.dynamic_update_slice(out, m, (offset, 0))


def _regroup_w_in(w_in):
    d = w_in.shape[0]
    za = w_in[:, 0:A_MAIN]
    zb = w_in[:, 1792:3840]
    zc = w_in[:, 3840:5888]
    igfg = w_in[:, 5888:5896]
    zd = w_in[:, 5896:6920]
    zg = w_in[:, 6920:11016]
    vr = w_in[:, 11016:] if w_in.shape[1] > 11016 else jnp.zeros((d, 32), w_in.dtype)
    small = jnp.concatenate([vr, igfg, jnp.zeros((d, LANES - 40), w_in.dtype)], axis=1)
    a_grp = jnp.concatenate([za, small, jnp.zeros((d, 2048 - A_MAIN - LANES), w_in.dtype)], axis=1)
    return jnp.concatenate([zg, zb, zc, a_grp, zd], axis=1).astype(BF16)


def _block_diag(w):
    g, di, dj = w.shape
    eye = jnp.eye(g, dtype=w.dtype)
    return (eye[:, None, :, None] * w[:, :, None, :]).reshape(g * di, g * dj)


def kernel(x, p, w_in_first, w_in_rest, rwkv_mu, rwkv_w0, rwkv_w2, rwkv_a0, rwkv_a2, rwkv_v0, rwkv_v2,
           rwkv_g2, rwkv_k_k, rwkv_k_a, rwkv_r_k, rwkv_lnx_g, rwkv_lnx_b, hgrn_lower_bounds, hgrn_norm_w,
           mlstm_conv_w, mlstm_conv_b, mlstm_i_bias, mlstm_f_bias, mlstm_norm_w, lru_conv_w, lru_conv_b,
           lru_wx, lru_bx, lru_wa, lru_ba, lru_lambda, w_branch, w_out, ln1_g, ln1_b, router_w, router_b,
           expert_w_gu, expert_b_gu, expert_w_down, expert_b_down, ple_gate_w, ple_proj_w, ln2_g, ln2_b):
    bsz, seq, _ = x.shape
    n = bsz * seq
    n_assign = n * TOP_K
    n_blocks = n_assign // EXPERT_ROWS + N_EXPERTS
    n_slots = n_blocks * EXPERT_ROWS

    lb_all = jnp.cumsum(jax.nn.softmax(hgrn_lower_bounds.astype(F32), axis=0), axis=0)
    lb_all = lb_all - lb_all[0]

    xf = x.reshape(n, D_MODEL).astype(F32)
    xb = xf.astype(BF16)
    v_first = jnp.zeros((n, W), F32)
    for layer in range(DEPTH):
        first = layer == 0
        w_in = _regroup_w_in(w_in_first if first else w_in_rest[layer - 1])
        z = _proj(xb, w_in)

        rw_prm = {
            "mu": _row(rwkv_mu[layer]),
            "w0": _row(rwkv_w0[layer]),
            "w2": _pad_rows(rwkv_w2[layer], LANES, 0).astype(BF16),
            "a0": _row(rwkv_a0[layer]),
            "a2": _pad_rows(rwkv_a2[layer], LANES, 64).astype(BF16),
            "v0": _row(jnp.zeros((W,), F32) if first else rwkv_v0[layer - 1]),
            "v2": (jnp.zeros((LANES, W), BF16) if first
                   else _pad_rows(rwkv_v2[layer - 1], LANES, SMALL_VR).astype(BF16)),
            "g2": rwkv_g2[layer].astype(BF16),
            "k_k": _row(rwkv_k_k[layer]),
            "k_a": _row(rwkv_k_a[layer]),
            "r_k": _row(rwkv_r_k[layer]),
            "lnx_g": _row(rwkv_lnx_g[layer]),
            "lnx_b": _row(rwkv_lnx_b[layer]),
        }
        y_a, v_cur = _rwkv(z, v_first, rw_prm, bsz, seq, first)
        if first:
            v_first = v_cur
        y_b = _hgrn(z, _row(lb_all[layer]), _row(hgrn_norm_w[layer]), bsz, seq)
        gate_bias = jnp.zeros((LANES,), F32)
        gate_bias = gate_bias.at[SMALL_IG:SMALL_IG + MLSTM_HEADS].set(mlstm_i_bias[layer])
        gate_bias = gate_bias.at[SMALL_FG:SMALL_FG + MLSTM_HEADS].set(mlstm_f_bias[layer])
        y_c = _mlstm(z, mlstm_conv_w[layer].astype(F32), _row(mlstm_conv_b[layer]), _row(gate_bias),
                     _row(mlstm_norm_w[layer]), bsz, seq)
        y_d = _lru(z, lru_conv_w[layer].astype(F32), _row(lru_conv_b[layer]),
                   _block_diag(lru_wx[layer]).astype(BF16), _row(lru_bx[layer]),
                   _block_diag(lru_wa[layer]).astype(BF16), _row(lru_ba[layer]), _row(lru_lambda[layer]),
                   bsz, seq)

        x1, base, x1p = _merge(xf, (y_a, y_b, y_c, y_d), z, w_branch[layer].astype(BF16), w_out[layer].astype(BF16),
                          _row(ln1_g[layer]), _row(ln1_b[layer]), p[layer].reshape(n, D_PLE),
                          ple_gate_w[layer].astype(BF16), ple_proj_w[layer].astype(BF16))

        rw = jnp.zeros((D_MODEL, LANES), F32).at[:, 0:N_EXPERTS].set(router_w[layer])
        rb = jnp.zeros((1, LANES), F32).at[0, 0:N_EXPERTS].set(router_b[layer])
        route, counts_f = _route(x1, rw, rb)

        idx = route[:, ROUTE_IDX:ROUTE_IDX + TOP_K].astype(I32)
        rank = route[:, ROUTE_RANK:ROUTE_RANK + TOP_K].astype(I32)
        counts = counts_f[0, 0:N_EXPERTS].astype(I32)
        padded = (counts + EXPERT_ROWS - 1) // EXPERT_ROWS * EXPERT_ROWS
        pad_end = jnp.cumsum(padded)
        pad_start = pad_end - padded
        dest = (pad_start[idx] + rank).reshape(-1)
        block_start = jnp.arange(n_blocks, dtype=I32) * EXPERT_ROWS
        block_e = jnp.minimum(jnp.sum((pad_end[None, :] <= block_start[:, None]).astype(I32), axis=1),
                              N_EXPERTS - 1)
        n_used = (pad_end[-1:] // EXPERT_ROWS).astype(I32)

        xs = _scatter(dest, x1p, n_slots)
        ys = _ffn(layer, block_e, n_used, xs, expert_w_gu.astype(F32),
                  expert_b_gu.reshape(DEPTH, N_EXPERTS, 1, 2 * D_EXPERT).astype(F32),
                  expert_w_down.astype(F32),
                  expert_b_down.reshape(DEPTH, N_EXPERTS, 1, D_MODEL).astype(F32))
        xf, xb = _combine(dest, base, route, ys, _row(ln2_g[layer]), _row(ln2_b[layer]))
    return xf.reshape(bsz, seq, D_MODEL).astype(x.dtype)
```

```python
import functools
import math

import jax
import jax.numpy as jnp
from jax import lax
from jax.experimental import pallas as pl
from jax.experimental.pallas import tpu as pltpu

F32 = jnp.float32
BF16 = jnp.bfloat16
I32 = jnp.int32

D_MODEL = 1024
DEPTH = 4
W = 512
RWKV_HEAD_DIM = 64
RWKV_LNX_EPS = 64e-5
MLSTM_HEADS = 4
HEAD128 = 128
CHUNK = 64
LRU_C = 8.0
CONV_WIDTH = 4
N_EXPERTS = 32
TOP_K = 4
D_EXPERT = 1024
SWIGLU_LIMIT = 7.0
SWIGLU_ALPHA = 1.702
D_PLE = 256
LN_EPS = 1e-5
NORM_EPS = 1e-6
ALPHA = (2 * DEPTH) ** 0.25

COL_G = 0
COL_B = 4096
COL_C = 6144
COL_A = 8192
COL_D = 10240
N_COLS = 11264
A_MAIN = 1792
SMALL_VR = 0
SMALL_IG = 32
SMALL_FG = 36

EXPERT_ROWS = 512
TOK_TILE = 256
LANES = 128
DMA_UNROLL = 8
VMEM_LIMIT = 48 * 1024 * 1024
FFN_VMEM_LIMIT = 56 * 1024 * 1024


def _cparams(sem):
    return pltpu.CompilerParams(dimension_semantics=sem, vmem_limit_bytes=VMEM_LIMIT)


def _mm(a, b):
    return jnp.dot(a.astype(BF16), b.astype(BF16), preferred_element_type=F32)


def _mm_nt(a, b):
    return lax.dot_general(a.astype(BF16), b.astype(BF16), (((1,), (1,)), ((), ())),
                           preferred_element_type=F32)


def _mm_tn(a, b):
    return jnp.dot(a.T.astype(BF16), b.astype(BF16), preferred_element_type=F32)


def _split3(x):
    hi = x.astype(BF16)
    r1 = x - hi.astype(F32)
    mid = r1.astype(BF16)
    lo = (r1 - mid.astype(F32)).astype(BF16)
    return hi, mid, lo


def _mm_exact_lhs(m, x):
    hi, mid, lo = _split3(x)
    return (jnp.dot(m, hi, preferred_element_type=F32) + jnp.dot(m, mid, preferred_element_type=F32)
            + jnp.dot(m, lo, preferred_element_type=F32))


def _mm_exact_rhs(x, m):
    hi = x.astype(BF16)
    lo = (x - hi.astype(F32)).astype(BF16)
    return jnp.dot(hi, m, preferred_element_type=F32) + jnp.dot(lo, m, preferred_element_type=F32)


def _iota(shape, dim):
    return lax.broadcasted_iota(I32, shape, dim)


def _tri_incl(n):
    return jnp.where(_iota((n, n), 1) <= _iota((n, n), 0), 1.0, 0.0).astype(BF16)


def _seg_ones(n, seg):
    return jnp.where(_iota((n, n), 0) // seg == _iota((n, n), 1) // seg, 1.0, 0.0).astype(BF16)


def _sigmoid(x):
    return jax.nn.sigmoid(x)


def _silu(x):
    return x * jax.nn.sigmoid(x)


def _log_sigmoid(x):
    return jnp.minimum(x, 0.0) - jnp.log1p(jnp.exp(-jnp.abs(x)))


U32 = jnp.uint32
HALF = D_MODEL // 2
ROW_SUB = HALF // LANES


def _pack_rows(x):
    bits = lax.bitcast_convert_type(x.astype(BF16).astype(F32), U32)
    return (bits[:, 0:HALF] >> 16) | bits[:, HALF:]


def _unpack_rows(w):
    lo = lax.bitcast_convert_type(w << 16, F32)
    hi = lax.bitcast_convert_type(w & jnp.uint32(0xFFFF0000), F32)
    return jnp.concatenate([lo, hi], axis=1)


def _layernorm_rows(x, g, b):
    xc = x - jnp.mean(x, -1, keepdims=True)
    var = jnp.mean(xc * xc, -1, keepdims=True)
    return xc * lax.rsqrt(var + LN_EPS) * g + b


def _proj_kernel(x_ref, w_ref, o_ref):
    o_ref[...] = jnp.dot(x_ref[...], w_ref[...], preferred_element_type=F32).astype(o_ref.dtype)


def _proj(xb, w):
    n, k = xb.shape
    c = w.shape[1]
    tm = min(2048, n)
    tn = 1024
    return pl.pallas_call(
        _proj_kernel,
        grid=(n // tm, c // tn),
        in_specs=[pl.BlockSpec((tm, k), lambda i, j: (i, 0)),
                  pl.BlockSpec((k, tn), lambda i, j: (0, j))],
        out_specs=pl.BlockSpec((tm, tn), lambda i, j: (i, j)),
        out_shape=jax.ShapeDtypeStruct((n, c), BF16),
        compiler_params=_cparams(("parallel", "parallel")),
        name="in_proj",
    )(xb, w)


RWKV_CHUNKS = 4
RWKV_T = CHUNK * RWKV_CHUNKS


def _rwkv_kernel(first, z_ref, vf_ref, mu_ref, w0_ref, w2_ref, a0_ref, a2_ref, v0_ref, v2_ref, g2_ref,
                 kk_ref, ka_ref, rk_ref, lng_ref, lnb_ref, y_ref, vo_ref, prev_ref, s_ref):
    t_len = RWKV_T

    @pl.when(pl.program_id(1) == 0)
    def _():
        prev_ref[...] = jnp.zeros_like(prev_ref)
        s_ref[...] = jnp.zeros_like(s_ref)

    z = z_ref[:, 0:A_MAIN].astype(F32)
    small = z_ref[:, A_MAIN:A_MAIN + LANES].astype(F32)
    row = _iota((t_len, 1), 0)
    zs = jnp.where(row == 0, prev_ref[...], pltpu.roll(z, 1, 0))
    prev_ref[...] = z[t_len - 1:t_len, :]
    zz = z + (zs - z) * mu_ref[...]

    r = zz[:, 0:W]
    k = zz[:, W:2 * W]
    v = zz[:, 2 * W:3 * W]
    xwa = zz[:, 3 * W:3 * W + LANES]
    xg = zz[:, 3 * W + LANES:3 * W + 2 * LANES]

    u = w0_ref[...] + _mm(jnp.tanh(xwa), w2_ref[...])
    logw = -math.exp(-0.5) * _sigmoid(u)
    a_sig = _sigmoid(a0_ref[...] + _mm(xwa, a2_ref[...]))
    g = _mm(_sigmoid(xg), g2_ref[...])
    if not first:
        v = v + (vf_ref[...] - v) * _sigmoid(v0_ref[...] + _mm(small, v2_ref[...]))
    vo_ref[...] = v

    seg = _seg_ones(LANES, RWKV_HEAD_DIM)
    tri = jnp.where((_iota((t_len, t_len), 1) <= _iota((t_len, t_len), 0))
                    & (_iota((t_len, t_len), 1) // CHUNK == _iota((t_len, t_len), 0) // CHUNK),
                    1.0, 0.0).astype(BF16)
    c = _mm_exact_lhs(tri, logw)
    gam = jnp.exp(c)
    igam = jnp.exp(-c)
    gam_ex = jnp.exp(c - logw)

    lane = _iota((1, LANES), 1)
    lane2 = _iota((1, 2 * LANES), 1)
    hm = [lane < RWKV_HEAD_DIM, lane >= RWKV_HEAD_DIM]
    hm2 = [lane2 % LANES < RWKV_HEAD_DIM, lane2 % LANES >= RWKV_HEAD_DIM]
    bd_mask = _iota((LANES, LANES), 0) // RWKV_HEAD_DIM == _iota((LANES, LANES), 1) // RWKV_HEAD_DIM
    pstrict = (_iota((CHUNK, LANES), 1) % RWKV_HEAD_DIM) < _iota((CHUNK, LANES), 0)
    pincl = (_iota((CHUNK, LANES), 1) % RWKV_HEAD_DIM) <= _iota((CHUNK, LANES), 0)

    def bdiag(m, masks=hm):
        return jnp.concatenate([jnp.where(masks[0], m, 0.0), jnp.where(masks[1], m, 0.0)], axis=0)

    n_pair = W // LANES
    sls = [slice(p * LANES, (p + 1) * LANES) for p in range(n_pair)]
    rss = [slice(ci * CHUNK, (ci + 1) * CHUNK) for ci in range(RWKV_CHUNKS)]
    kk_raw = [k[:, sl] * kk_ref[:, sl] for sl in sls]
    kkn = [kr / jnp.maximum(jnp.sqrt(_mm_exact_rhs(kr * kr, seg)), 1e-12) for kr in kk_raw]
    aps = [a_sig[:, sl] for sl in sls]
    k2s = [k[:, sl] * (1.0 + (ap - 1.0) * ka_ref[:, sl]) for sl, ap in zip(sls, aps)]
    r_ts = [r[:, sl] * gam[:, sl] for sl in sls]
    a_ts = [-kn * gam_ex[:, sl] for kn, sl in zip(kkn, sls)]
    kb = [kn * ap for kn, ap in zip(kkn, aps)]
    b_ts = [x_ * igam[:, sl] for x_, sl in zip(kb, sls)]
    k_ts = [k2 * igam[:, sl] for k2, sl in zip(k2s, sls)]

    cps = [(ci, p) for ci in range(RWKV_CHUNKS) for p in range(n_pair)]
    lhs = [jnp.concatenate([a_ts[p][rss[ci]], r_ts[p][rss[ci]]], axis=0) for ci, p in cps]
    pb = [_mm_nt(lhs[i], bdiag(b_ts[p][rss[ci]])) for i, (ci, p) in enumerate(cps)]
    pk = [_mm_nt(lhs[i], bdiag(k_ts[p][rss[ci]])) for i, (ci, p) in enumerate(cps)]
    v_bd = [bdiag(v[rss[ci], sls[p]]) for ci, p in cps]
    a_ak = [jnp.where(pstrict, m[0:CHUNK], 0.0) for m in pk]
    a_rb = [jnp.where(pincl, m[CHUNK:], 0.0) for m in pb]
    a_rk = [jnp.where(pincl, m[CHUNK:], 0.0) for m in pk]
    pw = [jnp.where(pstrict, m[0:CHUNK], 0.0) for m in pb]
    w2 = [jnp.concatenate([_mm(a_ak[i], v_bd[i]), a_ts[p][rss[ci]]], axis=1) for i, (ci, p) in enumerate(cps)]
    for it in range(6):
        w2 = [x_ + _mm(q_, bdiag(x_, hm2)) for x_, q_ in zip(w2, pw)]
        if it < 5:
            pw = [_mm(q_, bdiag(q_)) for q_ in pw]

    s_cur = [s_ref[p] for p in range(n_pair)]
    y_rows = [[] for _ in range(n_pair)]
    for ci in range(RWKV_CHUNKS):
        rs = rss[ci]
        idx = [ci * n_pair + p for p in range(n_pair)]
        c_last = c[ci * CHUNK + CHUNK - 1:ci * CHUNK + CHUNK, :]
        tail = jnp.exp(c_last - c[rs])
        g_last = jnp.exp(c_last)
        ls0 = [_mm_nt(jnp.concatenate([w2[idx[p]][:, LANES:], r_ts[p][rs]], axis=0), s_cur[p])
               for p in range(n_pair)]
        xs = [w2[idx[p]][:, 0:LANES] + ls0[p][0:CHUNK] for p in range(n_pair)]
        ys = [ls0[p][CHUNK:] + _mm(jnp.concatenate([a_rb[idx[p]], a_rk[idx[p]]], axis=1),
                                   jnp.concatenate([bdiag(xs[p]), v_bd[idx[p]]], axis=0))
              for p in range(n_pair)]
        upd = [_mm_tn(jnp.concatenate([xs[p], v[rs, sls[p]]], axis=0),
                      jnp.concatenate([kb[p][rs] * tail[:, sls[p]], k2s[p][rs] * tail[:, sls[p]]], axis=0))
               for p in range(n_pair)]
        s_cur = [s_cur[p] * g_last[:, sls[p]] + jnp.where(bd_mask, upd[p], 0.0) for p in range(n_pair)]
        for p in range(n_pair):
            y_rows[p].append(ys[p])
    for p in range(n_pair):
        s_ref[p] = s_cur[p]

    y_pairs = [jnp.concatenate(rows_, axis=0) for rows_ in y_rows]
    means = [_mm_exact_rhs(y_, seg) * (1.0 / RWKV_HEAD_DIM) for y_ in y_pairs]
    ycs = [y_ - m_ for y_, m_ in zip(y_pairs, means)]
    vars_ = [_mm_exact_rhs(yc * yc, seg) * (1.0 / RWKV_HEAD_DIM) for yc in ycs]
    bonus = [_mm_exact_rhs(r[:, sls[p]] * k2s[p] * rk_ref[:, sls[p]], seg) * v[:, sls[p]] for p in range(n_pair)]
    for p in range(n_pair):
        sl = sls[p]
        yn = ycs[p] * lax.rsqrt(vars_[p] + RWKV_LNX_EPS) * lng_ref[:, sl] + lnb_ref[:, sl]
        y_ref[:, sl] = ((yn + bonus[p]) * g[:, sl]).astype(y_ref.dtype)


def _rwkv(z, v_first, prm, bsz, seq, first):
    n = bsz * seq
    nj = seq // RWKV_T
    row_spec = lambda width: pl.BlockSpec((1, width), lambda b, j: (0, 0))
    mat_spec = lambda r_, c_: pl.BlockSpec((r_, c_), lambda b, j: (0, 0))
    tok = lambda width: pl.BlockSpec((RWKV_T, width), lambda b, j: (b * nj + j, 0))
    in_specs = [
        pl.BlockSpec((RWKV_T, 2048), lambda b, j: (b * nj + j, COL_A // 2048)),
        tok(W),
        row_spec(A_MAIN), row_spec(W), mat_spec(LANES, W), row_spec(W), mat_spec(LANES, W),
        row_spec(W), mat_spec(LANES, W), mat_spec(LANES, W),
        row_spec(W), row_spec(W), row_spec(W), row_spec(W), row_spec(W),
    ]
    return pl.pallas_call(
        functools.partial(_rwkv_kernel, first),
        grid=(bsz, nj),
        in_specs=in_specs,
        out_specs=[tok(W), tok(W)],
        out_shape=[jax.ShapeDtypeStruct((n, W), BF16), jax.ShapeDtypeStruct((n, W), F32)],
        scratch_shapes=[pltpu.VMEM((1, A_MAIN), F32), pltpu.VMEM((W // LANES, LANES, LANES), F32)],
        compiler_params=_cparams(("parallel", "arbitrary")),
        name="rwkv7",
    )(z, v_first, prm["mu"], prm["w0"], prm["w2"], prm["a0"], prm["a2"], prm["v0"], prm["v2"], prm["g2"],
      prm["k_k"], prm["k_a"], prm["r_k"], prm["lnx_g"], prm["lnx_b"])


HG_SUB = 16


def _hgrn_kernel(z_ref, lb_ref, nw_ref, y_ref, s_ref, q_s, k_s, v_s, bc_s, o_s):
    t_len = CHUNK

    @pl.when(pl.program_id(1) == 0)
    def _():
        s_ref[...] = jnp.zeros_like(s_ref)

    zq = z_ref[:, 0:W].astype(F32)
    f = z_ref[:, W:2 * W].astype(F32)
    lb = lb_ref[...]
    q_s[...] = _silu(zq)
    v_s[...] = z_ref[:, 2 * W:3 * W].astype(F32)
    la = jnp.log(lb)
    lbb = jnp.log1p(-lb) + _log_sigmoid(f)
    logf = jnp.maximum(la, lbb) + jnp.log1p(jnp.exp(-jnp.abs(la - lbb)))
    k_s[...] = (1.0 - lb) * _sigmoid(-f)
    blk = jnp.where((_iota((t_len, t_len), 1) <= _iota((t_len, t_len), 0))
                    & (_iota((t_len, t_len), 1) // HG_SUB == _iota((t_len, t_len), 0) // HG_SUB),
                    1.0, 0.0).astype(BF16)
    bc_s[...] = _mm_exact_lhs(blk, logf)

    row8 = _iota((HG_SUB // 2, 1), 0)

    def sub_block(i):
        r0 = i * HG_SUB
        n_head = W // HEAD128
        sls = [slice(h * HEAD128, (h + 1) * HEAD128) for h in range(n_head)]
        qb = q_s[pl.ds(r0, HG_SUB), :]
        kb = k_s[pl.ds(r0, HG_SUB), :]
        vb = v_s[pl.ds(r0, HG_SUB), :]
        bcb = bc_s[pl.ds(r0, HG_SUB), :]
        st = [s_ref[h] for h in range(n_head)]
        qd = qb * jnp.exp(bcb)
        o = [_mm_nt(qd[:, sl], st[h]) for h, sl in enumerate(sls)]
        half = HG_SUB // 2
        o_top = [x_[0:half] for x_ in o]
        o_bot = [x_[half:] for x_ in o]
        for s in range(HG_SUB):
            bs = bcb[s:s + 1, :]
            ks = kb[s:s + 1, :]
            if s < half:
                p_top = qb[0:half] * ks * jnp.exp(jnp.where(row8 >= s, bcb[0:half] - bs, -jnp.inf))
                p_bot = qb[half:] * ks * jnp.exp(bcb[half:] - bs)
            else:
                p_top = None
                p_bot = qb[half:] * ks * jnp.exp(jnp.where(row8 >= s - half, bcb[half:] - bs, -jnp.inf))
            for h, sl in enumerate(sls):
                vs = vb[s:s + 1, sl]
                if p_top is not None:
                    o_top[h] = o_top[h] + jnp.sum(p_top[:, sl], axis=-1, keepdims=True) * vs
                o_bot[h] = o_bot[h] + jnp.sum(p_bot[:, sl], axis=-1, keepdims=True) * vs
        bl = bcb[HG_SUB - 1:HG_SUB, :]
        kd = kb * jnp.exp(bl - bcb)
        e_last = jnp.exp(bl)
        upd = [_mm_tn(vb[:, sl], kd[:, sl]) for sl in sls]
        for h, sl in enumerate(sls):
            s_ref[h] = st[h] * e_last[:, sl] + upd[h]
            o_s[pl.ds(r0, half), sl] = o_top[h]
            o_s[pl.ds(r0 + half, half), sl] = o_bot[h]

    for i in range(t_len // HG_SUB):
        sub_block(i)

    g = z_ref[:, 3 * W:4 * W].astype(F32)
    for h in range(W // HEAD128):
        sl = slice(h * HEAD128, (h + 1) * HEAD128)
        o = o_s[:, sl]
        o = o * lax.rsqrt(jnp.mean(o * o, -1, keepdims=True) + NORM_EPS) * nw_ref[:, sl]
        y_ref[:, sl] = (o * _silu(g[:, sl])).astype(y_ref.dtype)


def _hgrn(z, lb, norm_w, bsz, seq):
    n = bsz * seq
    nj = seq // CHUNK
    row_spec = pl.BlockSpec((1, W), lambda b, j: (0, 0))
    return pl.pallas_call(
        _hgrn_kernel,
        grid=(bsz, nj),
        in_specs=[pl.BlockSpec((CHUNK, 2048), lambda b, j: (b * nj + j, COL_B // 2048)), row_spec, row_spec],
        out_specs=pl.BlockSpec((CHUNK, W), lambda b, j: (b * nj + j, 0)),
        out_shape=jax.ShapeDtypeStruct((n, W), BF16),
        scratch_shapes=[pltpu.VMEM((W // HEAD128, HEAD128, HEAD128), F32)]
        + [pltpu.VMEM((CHUNK, W), F32) for _ in range(5)],
        compiler_params=_cparams(("parallel", "arbitrary")),
        name="hgrn2",
    )(z, lb, norm_w)


def _mlstm_kernel(z_ref, sm_ref, cw_ref, cb_ref, gb_ref, nw_ref, y_ref, ext_ref, c_ref, n_ref, m_ref):
    t_len = CHUNK
    pad = 8

    @pl.when(pl.program_id(1) == 0)
    def _():
        ext_ref[0:pad, :] = jnp.zeros((pad, 2 * W), F32)
        c_ref[...] = jnp.zeros_like(c_ref)
        n_ref[...] = jnp.zeros_like(n_ref)
        m_ref[...] = jnp.full(m_ref.shape, -1e30, F32)

    ext_ref[pad:pad + t_len, :] = z_ref[:, 0:2 * W].astype(F32)
    acc = cb_ref[...]
    for jj in range(CONV_WIDTH):
        acc = acc + ext_ref[pl.ds(pad - (CONV_WIDTH - 1) + jj, t_len), :] * cw_ref[jj:jj + 1, :]
    ext_ref[0:pad, :] = ext_ref[t_len:t_len + pad, :]
    qk = _silu(acc)
    q = qk[:, 0:W]
    k = qk[:, W:2 * W] * (HEAD128 ** -0.5)
    v = z_ref[:, 2 * W:3 * W].astype(F32)
    og = z_ref[:, 3 * W:4 * W].astype(F32)

    gates = sm_ref[...].astype(F32) + gb_ref[...]
    lf = _log_sigmoid(gates)
    fc = _mm_exact_lhs(_tri_incl(t_len), lf)
    gates_t = gates.T
    fc_t = fc.T
    causal = _iota((t_len, t_len), 1) <= _iota((t_len, t_len), 0)

    hs = range(MLSTM_HEADS)
    sls = [slice(h * HEAD128, (h + 1) * HEAD128) for h in hs]
    ic_col = [gates[:, SMALL_IG + h:SMALL_IG + h + 1] for h in hs]
    ic_row = [gates_t[SMALL_IG + h:SMALL_IG + h + 1, :] for h in hs]
    fc_col = [fc[:, SMALL_FG + h:SMALL_FG + h + 1] for h in hs]
    fc_row = [fc_t[SMALL_FG + h:SMALL_FG + h + 1, :] for h in hs]
    m_prev = [m_ref[h:h + 1, 0:1] for h in hs]
    c_mat = [c_ref[h] for h in hs]
    n_vec = [n_ref[h:h + 1, :] for h in hs]
    qk_raw = [_mm_nt(q[:, sl], k[:, sl]) for sl in sls]
    q_c = [_mm(q[:, sl], c_mat[h]) for h, sl in enumerate(sls)]
    log_w = [jnp.where(causal, fc_col[h] - fc_row[h] + ic_row[h], -jnp.inf) for h in hs]
    log_inter = [fc_col[h] + m_prev[h] for h in hs]
    m_t = [jnp.maximum(log_inter[h], jnp.max(log_w[h], -1, keepdims=True)) for h in hs]
    w_inter = [jnp.exp(log_inter[h] - m_t[h]) for h in hs]
    s_qk = [qk_raw[h] * jnp.exp(log_w[h] - m_t[h]) for h in hs]
    s_v = [_mm(s_qk[h], v[:, sl]) for h, sl in enumerate(sls)]
    f_last = [fc_col[h][t_len - 1:t_len, :] for h in hs]
    log_s = [f_last[h] - fc_col[h] + ic_col[h] for h in hs]
    m_new = [jnp.maximum(f_last[h] + m_prev[h], jnp.max(log_s[h], 0, keepdims=True)) for h in hs]
    kw = [k[:, sl] * jnp.exp(log_s[h] - m_new[h]) for h, sl in enumerate(sls)]
    kv = [_mm_tn(kw[h], v[:, sl]) for h, sl in enumerate(sls)]
    for h, sl in enumerate(sls):
        num = w_inter[h] * q_c[h] + s_v[h]
        den = (w_inter[h] * jnp.sum(q[:, sl] * n_vec[h], -1, keepdims=True)
               + jnp.sum(s_qk[h], -1, keepdims=True))
        hid = num / jnp.maximum(jnp.abs(den), jnp.exp(-m_t[h]))
        decay = jnp.exp(f_last[h] + m_prev[h] - m_new[h])
        c_ref[h] = decay * c_mat[h] + kv[h]
        n_ref[h:h + 1, :] = decay * n_vec[h] + jnp.sum(kw[h], 0, keepdims=True)
        m_ref[h:h + 1, :] = jnp.broadcast_to(m_new[h], (1, LANES))
        hc = hid - jnp.mean(hid, -1, keepdims=True)
        hn = hc * lax.rsqrt(jnp.mean(hc * hc, -1, keepdims=True) + NORM_EPS) * nw_ref[:, sl]
        y_ref[:, sl] = (_sigmoid(og[:, sl]) * hn).astype(y_ref.dtype)


def _mlstm(z, conv_w, conv_b, gate_bias, norm_w, bsz, seq):
    n = bsz * seq
    nj = seq // CHUNK
    const = lambda r_, c_: pl.BlockSpec((r_, c_), lambda b, j: (0, 0))
    return pl.pallas_call(
        _mlstm_kernel,
        grid=(bsz, nj),
        in_specs=[pl.BlockSpec((CHUNK, 2048), lambda b, j: (b * nj + j, COL_C // 2048)),
                  pl.BlockSpec((CHUNK, LANES), lambda b, j: (b * nj + j, (COL_A + A_MAIN) // LANES)),
                  const(CONV_WIDTH, 2 * W), const(1, 2 * W), const(1, LANES), const(1, W)],
        out_specs=pl.BlockSpec((CHUNK, W), lambda b, j: (b * nj + j, 0)),
        out_shape=jax.ShapeDtypeStruct((n, W), BF16),
        scratch_shapes=[pltpu.VMEM((CHUNK + 8, 2 * W), F32),
                        pltpu.VMEM((MLSTM_HEADS, HEAD128, HEAD128), F32),
                        pltpu.VMEM((8, HEAD128), F32), pltpu.VMEM((8, LANES), F32)],
        compiler_params=_cparams(("parallel", "arbitrary")),
        name="mlstm",
    )(z, z, conv_w, conv_b, gate_bias, norm_w)


LRU_T = 256


def _lru_kernel(z_ref, cw_ref, cb_ref, wx_ref, bx_ref, wa_ref, ba_ref, lam_ref, y_ref, ext_ref, h_ref):
    t_len = LRU_T
    pad = 8
    first_blk = pl.program_id(1) == 0

    @pl.when(first_blk)
    def _():
        ext_ref[0:pad, :] = jnp.zeros((pad, W), F32)
        h_ref[...] = jnp.zeros_like(h_ref)

    ext_ref[pad:pad + t_len, :] = z_ref[:, 0:W].astype(F32)
    xc = cb_ref[...]
    for jj in range(CONV_WIDTH):
        xc = xc + ext_ref[pl.ds(pad - (CONV_WIDTH - 1) + jj, t_len), :] * cw_ref[jj:jj + 1, :]
    ext_ref[0:pad, :] = ext_ref[t_len:t_len + pad, :]

    gate_x = _sigmoid(_mm(xc, wx_ref[...]) + bx_ref[...])
    gate_a = _sigmoid(_mm(xc, wa_ref[...]) + ba_ref[...])
    log_a = LRU_C * gate_a * _log_sigmoid(lam_ref[...])
    a = jnp.exp(log_a)
    mult = jnp.sqrt(-jnp.tanh(log_a) * (a * a + 1.0))
    row = _iota((t_len, 1), 0)
    mult = jnp.where(jnp.logical_and(first_blk, row == 0), 1.0, mult)
    b = mult * gate_x * xc

    d = 1
    while d < t_len:
        keep = row >= d
        a_sh = jnp.where(keep, pltpu.roll(a, d, 0), 1.0)
        b_sh = jnp.where(keep, pltpu.roll(b, d, 0), 0.0)
        b = a * b_sh + b
        a = a * a_sh
        d *= 2
    h = a * h_ref[...] + b
    h_ref[...] = h[t_len - 1:t_len, :]
    y_ref[...] = (h * jax.nn.gelu(z_ref[:, W:2 * W].astype(F32), approximate=True)).astype(y_ref.dtype)


def _lru(z, conv_w, conv_b, wx, bx, wa, ba, lam, bsz, seq):
    n = bsz * seq
    nj = seq // LRU_T
    const = lambda r_, c_: pl.BlockSpec((r_, c_), lambda b, j: (0, 0))
    return pl.pallas_call(
        _lru_kernel,
        grid=(bsz, nj),
        in_specs=[pl.BlockSpec((LRU_T, 2 * W), lambda b, j: (b * nj + j, COL_D // (2 * W))),
                  const(CONV_WIDTH, W), const(1, W), const(W, W), const(1, W), const(W, W), const(1, W),
                  const(1, W)],
        out_specs=pl.BlockSpec((LRU_T, W), lambda b, j: (b * nj + j, 0)),
        out_shape=jax.ShapeDtypeStruct((n, W), BF16),
        scratch_shapes=[pltpu.VMEM((LRU_T + 8, W), F32), pltpu.VMEM((1, W), F32)],
        compiler_params=_cparams(("parallel", "arbitrary")),
        name="rglru",
    )(z, conv_w, conv_b, wx, bx, wa, ba, lam)


def _merge_kernel(x_ref, ya_ref, yb_ref, yc_ref, yd_ref, zg_ref, wb_ref, wo_ref, g1_ref, b1_ref,
                  p_ref, wpg_ref, wpp_ref, x1_ref, base_ref, x1p_ref):
    merged = None
    for nb, y_ref in enumerate((ya_ref, yb_ref, yc_ref, yd_ref)):
        proj = jnp.dot(y_ref[...], wb_ref[nb], preferred_element_type=F32)
        term = _sigmoid(zg_ref[:, nb * D_MODEL:(nb + 1) * D_MODEL].astype(F32)) * proj
        merged = term if merged is None else merged + term
    mix = _mm(merged, wo_ref[...])
    x1 = _layernorm_rows(ALPHA * x_ref[...] + mix, g1_ref[...], b1_ref[...])
    x1_ref[...] = x1
    x1p_ref[...] = _pack_rows(x1).reshape(x1.shape[0], ROW_SUB, LANES)
    ple = _sigmoid(_mm(x1, wpg_ref[...])) * _mm(p_ref[...], wpp_ref[...])
    base_ref[...] = ALPHA * x1 + ple


def _merge(x, ys, z, w_branch, w_out, g1, b1, p_l, w_pg, w_pp):
    n = x.shape[0]
    tm = TOK_TILE
    tok = lambda width: pl.BlockSpec((tm, width), lambda i: (i, 0))
    const2 = lambda r_, c_: pl.BlockSpec((r_, c_), lambda i: (0, 0))
    return pl.pallas_call(
        _merge_kernel,
        grid=(n // tm,),
        in_specs=[tok(D_MODEL), tok(W), tok(W), tok(W), tok(W),
                  pl.BlockSpec((tm, 4 * D_MODEL), lambda i: (i, COL_G // (4 * D_MODEL))),
                  pl.BlockSpec((4, W, D_MODEL), lambda i: (0, 0, 0)), const2(D_MODEL, D_MODEL),
                  const2(1, D_MODEL), const2(1, D_MODEL), tok(D_PLE), const2(D_MODEL, D_MODEL),
                  const2(D_PLE, D_MODEL)],
        out_specs=[tok(D_MODEL), tok(D_MODEL), pl.BlockSpec((tm, ROW_SUB, LANES), lambda i: (i, 0, 0))],
        out_shape=[jax.ShapeDtypeStruct((n, D_MODEL), F32), jax.ShapeDtypeStruct((n, D_MODEL), F32),
                   jax.ShapeDtypeStruct((n, ROW_SUB, LANES), U32)],
        compiler_params=_cparams(("parallel",)),
        name="merge_ln1",
    )(x, *ys, z, w_branch, w_out, g1, b1, p_l, w_pg, w_pp)


ROUTE_IDX = 0
ROUTE_RANK = 4
ROUTE_GATE = 8


def _route_kernel(x_ref, rw_ref, rb_ref, out_ref, cnt_ref, carry_ref):
    tm = x_ref.shape[0]

    @pl.when(pl.program_id(0) == 0)
    def _():
        carry_ref[...] = jnp.zeros_like(carry_ref)

    xh, xm, _ = _split3(x_ref[...])
    wh, wm, _ = _split3(rw_ref[...])
    logits = (jnp.dot(xh, wh, preferred_element_type=F32) + jnp.dot(xh, wm, preferred_element_type=F32)
              + jnp.dot(xm, wh, preferred_element_type=F32)) + rb_ref[...]
    lane = _iota((tm, LANES), 1)
    lane_f = lane.astype(F32)
    cur = jnp.where(lane < N_EXPERTS, logits, -jnp.inf)
    vals, idxs, hots = [], [], []
    for _k in range(TOP_K):
        m = jnp.max(cur, -1, keepdims=True)
        ik = jnp.min(jnp.where(cur == m, lane_f, float(LANES)), -1, keepdims=True)
        hot = lane_f == ik
        vals.append(m)
        idxs.append(ik)
        hots.append(hot)
        cur = jnp.where(hot, -jnp.inf, cur)
    exps = [jnp.exp(vk - vals[0]) for vk in vals]
    denom = exps[0] + exps[1] + exps[2] + exps[3]
    count = jnp.zeros((tm, LANES), F32)
    for hot in hots:
        count = count + jnp.where(hot, 1.0, 0.0)
    strict = jnp.where(_iota((tm, tm), 1) < _iota((tm, tm), 0), 1.0, 0.0).astype(BF16)
    before = jnp.dot(strict, count.astype(BF16), preferred_element_type=F32) + carry_ref[...]
    out = jnp.zeros((tm, LANES), F32)
    for kk in range(TOP_K):
        rank = jnp.sum(jnp.where(hots[kk], before, 0.0), -1, keepdims=True)
        out = jnp.where(lane == ROUTE_IDX + kk, idxs[kk], out)
        out = jnp.where(lane == ROUTE_RANK + kk, rank, out)
        out = jnp.where(lane == ROUTE_GATE + kk, exps[kk] / denom, out)
    out_ref[...] = out
    total = carry_ref[...] + jnp.sum(count, 0, keepdims=True)
    carry_ref[...] = total
    cnt_ref[...] = total


def _route(x1, rw, rb):
    n = x1.shape[0]
    tm = TOK_TILE
    return pl.pallas_call(
        _route_kernel,
        grid=(n // tm,),
        in_specs=[pl.BlockSpec((tm, D_MODEL), lambda i: (i, 0)),
                  pl.BlockSpec((D_MODEL, LANES), lambda i: (0, 0)),
                  pl.BlockSpec((1, LANES), lambda i: (0, 0))],
        out_specs=[pl.BlockSpec((tm, LANES), lambda i: (i, 0)), pl.BlockSpec((1, LANES), lambda i: (0, 0))],
        out_shape=[jax.ShapeDtypeStruct((n, LANES), F32), jax.ShapeDtypeStruct((1, LANES), F32)],
        scratch_shapes=[pltpu.VMEM((1, LANES), F32)],
        compiler_params=_cparams(("arbitrary",)),
        name="router",
    )(x1, rw, rb)


def _row_copy(src, dst, sem):
    return pltpu.make_async_copy(src, dst, sem)


def _scatter_kernel(dest_ref, x_ref, xs_in_ref, xs_ref, sem):
    del xs_in_ref
    tm = x_ref.shape[0]

    def start(n, carry):
        for kk in range(TOP_K):
            d = dest_ref[n * TOP_K + kk]
            _row_copy(x_ref.at[n], xs_ref.at[d], sem).start(priority=kk % 2)
        return carry

    lax.fori_loop(0, tm, start, 0, unroll=DMA_UNROLL)

    def wait(n, carry):
        for kk in range(TOP_K):
            _row_copy(x_ref.at[0], xs_ref.at[0], sem).wait()
        return carry

    lax.fori_loop(0, tm, wait, 0, unroll=DMA_UNROLL)


def _scatter(dest, x1p, n_slots):
    n = x1p.shape[0]
    tm = TOK_TILE
    zeros = jnp.zeros((n_slots, ROW_SUB, LANES), U32)
    return pl.pallas_call(
        _scatter_kernel,
        grid=(n // tm,),
        in_specs=[pl.BlockSpec((tm * TOP_K,), lambda i: (i,), memory_space=pltpu.SMEM),
                  pl.BlockSpec((tm, ROW_SUB, LANES), lambda i: (i, 0, 0)),
                  pl.BlockSpec(memory_space=pl.ANY)],
        out_specs=pl.BlockSpec(memory_space=pl.ANY),
        out_shape=jax.ShapeDtypeStruct((n_slots, ROW_SUB, LANES), U32),
        scratch_shapes=[pltpu.SemaphoreType.DMA(())],
        input_output_aliases={2: 0},
        compiler_params=_cparams(("arbitrary",)),
        name="moe_scatter",
    )(dest, x1p, zeros)


def _ffn_kernel(be_ref, nu_ref, xs_ref, wgu_ref, bgu_ref, wd_ref, bd_ref, ys_ref, wgu_b, wd_b):
    i = pl.program_id(0)
    live = i < nu_ref[0]
    fresh = jnp.logical_or(i == 0, be_ref[i] != be_ref[jnp.maximum(i - 1, 0)])

    @pl.when(jnp.logical_and(live, fresh))
    def _():
        wgu_b[...] = wgu_ref[0, 0].astype(BF16)
        wd_b[...] = wd_ref[0, 0].astype(BF16)

    @pl.when(live)
    def _():
        gu = jnp.dot(_unpack_rows(xs_ref[...].reshape(EXPERT_ROWS, HALF)).astype(BF16), wgu_b[...], preferred_element_type=F32) + bgu_ref[0, 0]
        gate = jnp.minimum(gu[:, 0:D_EXPERT], SWIGLU_LIMIT)
        up = jnp.clip(gu[:, D_EXPERT:], -SWIGLU_LIMIT, SWIGLU_LIMIT)
        glu = gate * _sigmoid(SWIGLU_ALPHA * gate)
        act = ((up + 1.0) * glu).astype(BF16)
        y = jnp.dot(act, wd_b[...], preferred_element_type=F32) + bd_ref[0, 0]
        ys_ref[...] = _pack_rows(y).reshape(EXPERT_ROWS, ROW_SUB, LANES)

    @pl.when(jnp.logical_not(live))
    def _():
        ys_ref[...] = jnp.zeros_like(ys_ref)


def _ffn(layer, block_e, n_used, xs, w_gu, b_gu, w_down, b_down):
    n_slots = xs.shape[0]
    bm = EXPERT_ROWS
    grid_spec = pltpu.PrefetchScalarGridSpec(
        num_scalar_prefetch=2,
        grid=(n_slots // bm,),
        in_specs=[pl.BlockSpec((bm, ROW_SUB, LANES), lambda i, be, nu: (i, 0, 0)),
                  pl.BlockSpec((1, 1, D_MODEL, 2 * D_EXPERT), lambda i, be, nu: (layer, be[i], 0, 0)),
                  pl.BlockSpec((1, 1, 1, 2 * D_EXPERT), lambda i, be, nu: (layer, be[i], 0, 0)),
                  pl.BlockSpec((1, 1, D_EXPERT, D_MODEL), lambda i, be, nu: (layer, be[i], 0, 0)),
                  pl.BlockSpec((1, 1, 1, D_MODEL), lambda i, be, nu: (layer, be[i], 0, 0))],
        out_specs=pl.BlockSpec((bm, ROW_SUB, LANES), lambda i, be, nu: (i, 0, 0)),
        scratch_shapes=[pltpu.VMEM((D_MODEL, 2 * D_EXPERT), BF16), pltpu.VMEM((D_EXPERT, D_MODEL), BF16)],
    )
    return pl.pallas_call(
        _ffn_kernel,
        grid_spec=grid_spec,
        out_shape=jax.ShapeDtypeStruct((n_slots, ROW_SUB, LANES), U32),
        compiler_params=pltpu.CompilerParams(dimension_semantics=("arbitrary",),
                                             vmem_limit_bytes=FFN_VMEM_LIMIT),
        name="moe_ffn",
    )(block_e, n_used, xs, w_gu, b_gu, w_down, b_down)


def _combine_kernel(dest_ref, dnext_ref, base_ref, route_ref, ys_ref, g2_ref, b2_ref, xo_ref, xb_ref, buf, sems):
    tm = base_ref.shape[0]
    step = pl.program_id(0)
    slot = step % 2

    def issue(idx_ref, s):
        def start(n, carry):
            for kk in range(TOP_K):
                d = idx_ref[n * TOP_K + kk]
                _row_copy(ys_ref.at[d], buf.at[s, kk, n], sems.at[s]).start(priority=kk % 2)
            return carry

        lax.fori_loop(0, tm, start, 0, unroll=DMA_UNROLL)

    @pl.when(step == 0)
    def _():
        issue(dest_ref, 0)

    @pl.when(step + 1 < pl.num_programs(0))
    def _():
        issue(dnext_ref, 1 - slot)

    def wait(n, carry):
        for kk in range(TOP_K):
            _row_copy(ys_ref.at[0], buf.at[slot, 0, 0], sems.at[slot]).wait()
        return carry

    lax.fori_loop(0, tm, wait, 0, unroll=DMA_UNROLL)

    acc = base_ref[...]
    for kk in range(TOP_K):
        acc = acc + route_ref[:, ROUTE_GATE + kk:ROUTE_GATE + kk + 1] * _unpack_rows(buf[slot, kk].reshape(tm, HALF))
    out = _layernorm_rows(acc, g2_ref[...], b2_ref[...])
    xo_ref[...] = out
    xb_ref[...] = out.astype(BF16)


def _combine(dest, base, route, ys, g2, b2):
    n = base.shape[0]
    tm = TOK_TILE
    tok = lambda width: pl.BlockSpec((tm, width), lambda i: (i, 0))
    n_steps = n // tm
    return pl.pallas_call(
        _combine_kernel,
        grid=(n_steps,),
        in_specs=[pl.BlockSpec((tm * TOP_K,), lambda i: (i,), memory_space=pltpu.SMEM),
                  pl.BlockSpec((tm * TOP_K,), lambda i: (jnp.minimum(i + 1, n_steps - 1),),
                               memory_space=pltpu.SMEM),
                  tok(D_MODEL), tok(LANES), pl.BlockSpec(memory_space=pl.ANY),
                  pl.BlockSpec((1, D_MODEL), lambda i: (0, 0)), pl.BlockSpec((1, D_MODEL), lambda i: (0, 0))],
        out_specs=[tok(D_MODEL), tok(D_MODEL)],
        out_shape=[jax.ShapeDtypeStruct((n, D_MODEL), F32), jax.ShapeDtypeStruct((n, D_MODEL), BF16)],
        scratch_shapes=[pltpu.VMEM((2, TOP_K, tm, ROW_SUB, LANES), U32), pltpu.SemaphoreType.DMA((2,))],
        compiler_params=_cparams(("arbitrary",)),
        name="moe_combine_ln2",
    )(dest, dest, base, route, ys, g2, b2)


def _row(v):
    return v.reshape(1, -1).astype(F32)


def _pad_rows(m, rows, offset):
    out = jnp.zeros((rows, m.shape[1]), m.dtype)
    return lax.dynamic_update_slice(out, m, (offset, 0))


def _regroup_w_in(w_in):
    d = w_in.shape[0]
    za = w_in[:, 0:A_MAIN]
    zb = w_in[:, 1792:3840]
    zc = w_in[:, 3840:5888]
    igfg = w_in[:, 5888:5896]
    zd = w_in[:, 5896:6920]
    zg = w_in[:, 6920:11016]
    vr = w_in[:, 11016:] if w_in.shape[1] > 11016 else jnp.zeros((d, 32), w_in.dtype)
    small = jnp.concatenate([vr, igfg, jnp.zeros((d, LANES - 40), w_in.dtype)], axis=1)
    a_grp = jnp.concatenate([za, small, jnp.zeros((d, 2048 - A_MAIN - LANES), w_in.dtype)], axis=1)
    return jnp.concatenate([zg, zb, zc, a_grp, zd], axis=1).astype(BF16)


def _block_diag(w):
    g, di, dj = w.shape
    eye = jnp.eye(g, dtype=w.dtype)
    return (eye[:, None, :, None] * w[:, :, None, :]).reshape(g * di, g * dj)


def kernel(x, p, w_in_first, w_in_rest, rwkv_mu, rwkv_w0, rwkv_w2, rwkv_a0, rwkv_a2, rwkv_v0, rwkv_v2,
           rwkv_g2, rwkv_k_k, rwkv_k_a, rwkv_r_k, rwkv_lnx_g, rwkv_lnx_b, hgrn_lower_bounds, hgrn_norm_w,
           mlstm_conv_w, mlstm_conv_b, mlstm_i_bias, mlstm_f_bias, mlstm_norm_w, lru_conv_w, lru_conv_b,
           lru_wx, lru_bx, lru_wa, lru_ba, lru_lambda, w_branch, w_out, ln1_g, ln1_b, router_w, router_b,
           expert_w_gu, expert_b_gu, expert_w_down, expert_b_down, ple_gate_w, ple_proj_w, ln2_g, ln2_b):
    bsz, seq, _ = x.shape
    n = bsz * seq
    n_assign = n * TOP_K
    n_blocks = n_assign // EXPERT_ROWS + N_EXPERTS
    n_slots = n_blocks * EXPERT_ROWS

    lb_all = jnp.cumsum(jax.nn.softmax(hgrn_lower_bounds.astype(F32), axis=0), axis=0)
    lb_all = lb_all - lb_all[0]

    xf = x.reshape(n, D_MODEL).astype(F32)
    xb = xf.astype(BF16)
    v_first = jnp.zeros((n, W), F32)
    for layer in range(DEPTH):
        first = layer == 0
        w_in = _regroup_w_in(w_in_first if first else w_in_rest[layer - 1])
        z = _proj(xb, w_in)

        rw_prm = {
            "mu": _row(rwkv_mu[layer]),
            "w0": _row(rwkv_w0[layer]),
            "w2": _pad_rows(rwkv_w2[layer], LANES, 0).astype(BF16),
            "a0": _row(rwkv_a0[layer]),
            "a2": _pad_rows(rwkv_a2[layer], LANES, 64).astype(BF16),
            "v0": _row(jnp.zeros((W,), F32) if first else rwkv_v0[layer - 1]),
            "v2": (jnp.zeros((LANES, W), BF16) if first
                   else _pad_rows(rwkv_v2[layer - 1], LANES, SMALL_VR).astype(BF16)),
            "g2": rwkv_g2[layer].astype(BF16),
            "k_k": _row(rwkv_k_k[layer]),
            "k_a": _row(rwkv_k_a[layer]),
            "r_k": _row(rwkv_r_k[layer]),
            "lnx_g": _row(rwkv_lnx_g[layer]),
            "lnx_b": _row(rwkv_lnx_b[layer]),
        }
        y_a, v_cur = _rwkv(z, v_first, rw_prm, bsz, seq, first)
        if first:
            v_first = v_cur
        y_b = _hgrn(z, _row(lb_all[layer]), _row(hgrn_norm_w[layer]), bsz, seq)
        gate_bias = jnp.zeros((LANES,), F32)
        gate_bias = gate_bias.at[SMALL_IG:SMALL_IG + MLSTM_HEADS].set(mlstm_i_bias[layer])
        gate_bias = gate_bias.at[SMALL_FG:SMALL_FG + MLSTM_HEADS].set(mlstm_f_bias[layer])
        y_c = _mlstm(z, mlstm_conv_w[layer].astype(F32), _row(mlstm_conv_b[layer]), _row(gate_bias),
                     _row(mlstm_norm_w[layer]), bsz, seq)
        y_d = _lru(z, lru_conv_w[layer].astype(F32), _row(lru_conv_b[layer]),
                   _block_diag(lru_wx[layer]).astype(BF16), _row(lru_bx[layer]),
                   _block_diag(lru_wa[layer]).astype(BF16), _row(lru_ba[layer]), _row(lru_lambda[layer]),
                   bsz, seq)

        x1, base, x1p = _merge(xf, (y_a, y_b, y_c, y_d), z, w_branch[layer].astype(BF16), w_out[layer].astype(BF16),
                          _row(ln1_g[layer]), _row(ln1_b[layer]), p[layer].reshape(n, D_PLE),
                          ple_gate_w[layer].astype(BF16), ple_proj_w[layer].astype(BF16))

        rw = jnp.zeros((D_MODEL, LANES), F32).at[:, 0:N_EXPERTS].set(router_w[layer])
        rb = jnp.zeros((1, LANES), F32).at[0, 0:N_EXPERTS].set(router_b[layer])
        route, counts_f = _route(x1, rw, rb)

        idx = route[:, ROUTE_IDX:ROUTE_IDX + TOP_K].astype(I32)
        rank = route[:, ROUTE_RANK:ROUTE_RANK + TOP_K].astype(I32)
        counts = counts_f[0, 0:N_EXPERTS].astype(I32)
        padded = (counts + EXPERT_ROWS - 1) // EXPERT_ROWS * EXPERT_ROWS
        pad_end = jnp.cumsum(padded)
        pad_start = pad_end - padded
        dest = (pad_start[idx] + rank).reshape(-1)
        block_start = jnp.arange(n_blocks, dtype=I32) * EXPERT_ROWS
        block_e = jnp.minimum(jnp.sum((pad_end[None, :] <= block_start[:, None]).astype(I32), axis=1),
                              N_EXPERTS - 1)
        n_used = (pad_end[-1:] // EXPERT_ROWS).astype(I32)

        xs = _scatter(dest, x1p, n_slots)
        ys = _ffn(layer, block_e, n_used, xs, expert_w_gu.astype(F32),
                  expert_b_gu.reshape(DEPTH, N_EXPERTS, 1, 2 * D_EXPERT).astype(F32),
                  expert_w_down.astype(F32),
                  expert_b_down.reshape(DEPTH, N_EXPERTS, 1, D_MODEL).astype(F32))
        xf, xb = _combine(dest, base, route, ys, _row(ln2_g[layer]), _row(ln2_b[layer]))
    return xf.reshape(bsz, seq, D_MODEL).astype(x.dtype)
```
